```python
import jax, jax.numpy as jnp
from jax import lax
import numpy as np

D_MODEL = 2048
BATCH = 16
SEQ = 2048
DEPTH = 4
DEC_BATCH = 2
DEC_SEQ = 4096
PAST_LEN = 128

HEAD_DIM = 128
N_HEADS_A = 8
N_HEADS_B = 8
N_KV_B = 2
D_A = N_HEADS_A * HEAD_DIM
D_B = N_HEADS_B * HEAD_DIM
D_KV_B = N_KV_B * HEAD_DIM
D_MIX = D_A + D_B
D_IN = 3 * D_A + D_B + 2 * D_KV_B
IN_SPLITS = (D_A, 2 * D_A, 3 * D_A, 3 * D_A + D_B, 3 * D_A + D_B + D_KV_B)
DILATED_PATTERNS = ((128, 1), (512, 4), (2048, 16))
SWA_RADIUS = 128
SWA_BLOCK = 128
N_GROUPS = 4
N_EXPERTS_PER_GROUP = 4
TOP_K_IN_GROUP = 2
D_FF_EXPERT = D_MODEL // 8
DEEPNORM_ALPHA = (2 * DEPTH) ** 0.25
DEEPNORM_BETA = (8 * DEPTH) ** -0.25
LN_EPS = 1e-5
NEG_INF = -1e30

kernel_name = "hymba_dilated_swa_hmoe_deepnorm_encoder"


def alibi_slopes():
    n = N_HEADS_A + N_HEADS_B
    s = jnp.asarray(2.0 ** (-8.0 * np.arange(1, n + 1, dtype=np.float32) / n), jnp.float32)
    return s[0::2], s[1::2]


def layer_norm(x, g, b):
    xf = x.astype(jnp.float32)
    mu = jnp.mean(xf, axis=-1, keepdims=True)
    var = jnp.mean(jnp.square(xf - mu), axis=-1, keepdims=True)
    return ((xf - mu) * lax.rsqrt(var + LN_EPS) * g.astype(jnp.float32) + b.astype(jnp.float32)).astype(x.dtype)


def rms_norm(x, g):
    xf = x.astype(jnp.float32)
    return (xf * lax.rsqrt(jnp.mean(jnp.square(xf), axis=-1, keepdims=True) + LN_EPS) * g.astype(jnp.float32)).astype(x.dtype)


def _halo(t, nb, block):
    B, L, H, hd = t.shape
    tb = t.reshape(B, nb, block, H, hd)
    zero = jnp.zeros_like(tb[:, :1])
    prev = jnp.concatenate([zero, tb[:, :-1]], axis=1)
    nxt = jnp.concatenate([tb[:, 1:], zero], axis=1)
    return jnp.concatenate([prev, tb, nxt], axis=2)


def band_attention(q, k, v, slopes, radius, block, dist_scale, n_valid, sink_logit):
    B, L, Hq, hd = q.shape
    Hkv = k.shape[2]
    rep = Hq // Hkv
    nb = L // block
    qb = q.reshape(B, nb, block, Hkv, rep, hd)
    kb = _halo(k, nb, block)
    vb = _halo(v, nb, block)
    s = jnp.einsum('bnqgrd,bnkgd->bngrqk', qb, kb, preferred_element_type=jnp.float32) * (hd ** -0.5)
    qpos = jnp.arange(L).reshape(nb, block)
    kpos = (jnp.arange(nb)[:, None] - 1) * block + jnp.arange(3 * block)[None, :]
    rel = kpos[:, None, :] - qpos[:, :, None]
    valid = (jnp.abs(rel) <= radius) & (kpos[:, None, :] >= 0) & (kpos[:, None, :] < n_valid)
    dist = (jnp.abs(rel) * dist_scale).astype(jnp.float32)
    bias = -slopes.reshape(1, 1, Hkv, rep, 1, 1) * dist[None, :, None, None]
    s = jnp.where(valid[None, :, None, None], s + bias, NEG_INF)
    lse = jax.nn.logsumexp(s, axis=-1)
    if sink_logit is not None:
        lse = jnp.logaddexp(lse, sink_logit.astype(jnp.float32).reshape(1, 1, Hkv, rep, 1))
    p = jnp.exp(s - lse[..., None])
    o = jnp.einsum('bngrqk,bnkgd->bnqgrd', p.astype(v.dtype), vb).reshape(B, L, Hq, hd)
    lse = lse.transpose(0, 1, 4, 2, 3).reshape(B, L, Hq)
    return o, lse


def dilated_attention(q, k, v, slopes):
    B, S, H, hd = q.shape
    outs, lses = [], []
    for window, dil in DILATED_PATTERNS:
        n_side = window // (2 * dil)
        L = S // dil
        Lp = -(-L // n_side) * n_side

        def to_res(t):
            t = t.reshape(B, L, dil, H, hd).transpose(0, 2, 1, 3, 4).reshape(B * dil, L, H, hd)
            return jnp.pad(t, ((0, 0), (0, Lp - L), (0, 0), (0, 0)))

        o, lse = band_attention(to_res(q), to_res(k), to_res(v), slopes, n_side, n_side, dil, L, None)
        outs.append(o[:, :L].reshape(B, dil, L, H, hd).transpose(0, 2, 1, 3, 4).reshape(B, S, H, hd))
        lses.append(lse[:, :L].reshape(B, dil, L, H).transpose(0, 2, 1, 3).reshape(B, S, H))
    w = jax.nn.softmax(jnp.stack(lses, axis=0), axis=0)
    return jnp.einsum('pbsh,pbshd->bshd', w.astype(q.dtype), jnp.stack(outs, axis=0))


def hierarchical_moe(h, w_rg, b_rg, w_re, b_re, w_gate, w_up, w_down):
    gl = jnp.einsum('bsd,dg->bsg', h, w_rg, preferred_element_type=jnp.float32) + b_rg
    g_prob = jax.nn.softmax(gl, axis=-1)
    g_onehot = jax.nn.one_hot(jnp.argmax(gl, axis=-1), N_GROUPS, dtype=jnp.float32)
    g_gate = jnp.sum(g_prob * g_onehot, axis=-1)
    el = jnp.einsum('bsd,gde->bsge', h, w_re, preferred_element_type=jnp.float32) + b_re
    el_sel = jnp.einsum('bsge,bsg->bse', el, g_onehot)
    top_l, top_i = lax.top_k(el_sel, TOP_K_IN_GROUP)
    top_w = jax.nn.softmax(top_l, axis=-1)
    e_w = jnp.einsum('bsk,bske->bse', top_w, jax.nn.one_hot(top_i, N_EXPERTS_PER_GROUP, dtype=jnp.float32))
    combine = g_onehot[..., :, None] * (g_gate[..., None] * e_w)[..., None, :]
    a = jnp.einsum('bsd,gedf->bsgef', h, w_gate)
    u = jnp.einsum('bsd,gedf->bsgef', h, w_up)
    hid = jax.nn.silu(a) * u * combine[..., None].astype(h.dtype)
    return jnp.einsum('bsgef,gefd->bsd', hid, w_down)


def encoder_trunk(x, c, ln_in_g, ln_in_b, w_ada, b_ada, w_in, sink_b, gn_a, gn_b, w_out,
                  ln1_g, ln1_b, w_rg, b_rg, w_re, b_re, w_gate, w_up, w_down, ln2_g, ln2_b):
    Bsz, S, _ = x.shape
    slopes_a, slopes_b = alibi_slopes()
    x = layer_norm(x, ln_in_g, ln_in_b)
    for l in range(DEPTH):
        mod = jnp.einsum('bd,de->be', jax.nn.silu(c), w_ada[l]) + b_ada[l]
        sh1, sc1, g1, sh2, sc2, g2 = [m[:, None, :] for m in jnp.split(mod, 6, axis=-1)]
        h = x * (1 + sc1) + sh1
        proj = jnp.einsum('bsd,de->bse', h, w_in[l])
        qa, ka, va, qb, kb, vb = jnp.split(proj, IN_SPLITS, axis=-1)
        oa = dilated_attention(qa.reshape(Bsz, S, N_HEADS_A, HEAD_DIM),
                               ka.reshape(Bsz, S, N_HEADS_A, HEAD_DIM),
                               va.reshape(Bsz, S, N_HEADS_A, HEAD_DIM), slopes_a)
        ob, _ = band_attention(qb.reshape(Bsz, S, N_HEADS_B, HEAD_DIM),
                               kb.reshape(Bsz, S, N_KV_B, HEAD_DIM),
                               vb.reshape(Bsz, S, N_KV_B, HEAD_DIM),
                               slopes_b, SWA_RADIUS, SWA_BLOCK, 1, S, sink_b[l])
        mixed = jnp.concatenate([rms_norm(oa.reshape(Bsz, S, D_A), gn_a[l]),
                                 rms_norm(ob.reshape(Bsz, S, D_B), gn_b[l])], axis=-1)
        mix_out = jnp.einsum('bse,ed->bsd', mixed, w_out[l])
        x = layer_norm(DEEPNORM_ALPHA * x + (1 + g1) * mix_out, ln1_g[l], ln1_b[l])
        h2 = x * (1 + sc2) + sh2
        moe_out = hierarchical_moe(h2, w_rg[l], b_rg[l], w_re[l], b_re[l], w_gate[l], w_up[l], w_down[l])
        x = layer_norm(DEEPNORM_ALPHA * x + (1 + g2) * moe_out, ln2_g[l], ln2_b[l])
    return x


def setup_inputs(seed: int = 0) -> dict:
    key = jax.random.key(seed)
    ks = jax.random.split(key, 26)

    def nrm(k, shape, scale):
        return jax.random.normal(k, shape, jnp.float32) * scale

    G, E, F = N_GROUPS, N_EXPERTS_PER_GROUP, D_FF_EXPERT
    col_scale = jnp.concatenate([jnp.ones((2 * D_A,), jnp.float32),
                                 jnp.full((D_A,), DEEPNORM_BETA, jnp.float32),
                                 jnp.ones((D_B + D_KV_B,), jnp.float32),
                                 jnp.full((D_KV_B,), DEEPNORM_BETA, jnp.float32)])
    return {
        "x_prompt": nrm(ks[0], (BATCH, SEQ, D_MODEL), 1.0),
        "x_sample": nrm(ks[1], (DEC_BATCH, DEC_SEQ, D_MODEL), 1.0),
        "c_prompt": nrm(ks[2], (BATCH, D_MODEL), 1.0),
        "c_sample": nrm(ks[3], (DEC_BATCH, D_MODEL), 1.0),
        "ln_in_g": 1.0 + nrm(ks[4], (D_MODEL,), 0.02),
        "ln_in_b": nrm(ks[5], (D_MODEL,), 0.02),
        "w_ada": nrm(ks[6], (DEPTH, D_MODEL, 6 * D_MODEL), 0.1 * D_MODEL ** -0.5),
        "b_ada": nrm(ks[7], (DEPTH, 6 * D_MODEL), 0.02),
        "w_in": nrm(ks[8], (DEPTH, D_MODEL, D_IN), D_MODEL ** -0.5) * col_scale,
        "sink_b": nrm(ks[9], (DEPTH, N_HEADS_B), 0.5),
        "gn_a": 1.0 + nrm(ks[10], (DEPTH, D_A), 0.02),
        "gn_b": 1.0 + nrm(ks[11], (DEPTH, D_B), 0.02),
        "w_out": nrm(ks[12], (DEPTH, D_MIX, D_MODEL), DEEPNORM_BETA * D_MIX ** -0.5),
        "ln1_g": 1.0 + nrm(ks[13], (DEPTH, D_MODEL), 0.02),
        "ln1_b": nrm(ks[14], (DEPTH, D_MODEL), 0.02),
        "w_rg": nrm(ks[15], (DEPTH, D_MODEL, G), D_MODEL ** -0.5),
        "b_rg": nrm(ks[16], (DEPTH, G), 0.01),
        "w_re": nrm(ks[17], (DEPTH, G, D_MODEL, E), D_MODEL ** -0.5),
        "b_re": nrm(ks[18], (DEPTH, G, E), 0.01),
        "w_gate": nrm(ks[19], (DEPTH, G, E, D_MODEL, F), D_MODEL ** -0.5),
        "w_up": nrm(ks[20], (DEPTH, G, E, D_MODEL, F), D_MODEL ** -0.5),
        "w_down": nrm(ks[21], (DEPTH, G, E, F, D_MODEL), DEEPNORM_BETA * F ** -0.5),
        "ln2_g": 1.0 + nrm(ks[22], (DEPTH, D_MODEL), 0.02),
        "ln2_b": nrm(ks[23], (DEPTH, D_MODEL), 0.02),
    }


def reference(x_prompt, x_sample, c_prompt, c_sample, ln_in_g, ln_in_b, w_ada, b_ada, w_in,
              sink_b, gn_a, gn_b, w_out, ln1_g, ln1_b, w_rg, b_rg, w_re, b_re, w_gate, w_up,
              w_down, ln2_g, ln2_b):
    y_prompt = encoder_trunk(x_prompt, c_prompt, ln_in_g, ln_in_b, w_ada, b_ada, w_in, sink_b,
                             gn_a, gn_b, w_out, ln1_g, ln1_b, w_rg, b_rg, w_re, b_re,
                             w_gate, w_up, w_down, ln2_g, ln2_b)
    y_sample = encoder_trunk(x_sample, c_sample, ln_in_g, ln_in_b, w_ada, b_ada, w_in, sink_b,
                             gn_a, gn_b, w_out, ln1_g, ln1_b, w_rg, b_rg, w_re, b_re,
                             w_gate, w_up, w_down, ln2_g, ln2_b)
    return (y_prompt, y_sample)
```

```python
import functools

import numpy as np
import jax
import jax.numpy as jnp
from jax import lax
from jax.experimental import pallas as pl
from jax.experimental.pallas import tpu as pltpu

HEAD_DIM = 128
N_HEADS_A = 8
N_HEADS_B = 8
N_KV_B = 2
REP_B = N_HEADS_B // N_KV_B
D_A = N_HEADS_A * HEAD_DIM
D_B = N_HEADS_B * HEAD_DIM
D_KV_B = N_KV_B * HEAD_DIM
D_IN = 3 * D_A + D_B + 2 * D_KV_B
DILATED_PATTERNS = ((128, 1), (512, 4), (2048, 16))
SWA_RADIUS = 128
N_GROUPS = 4
N_EXPERTS_PER_GROUP = 4
N_EXPERTS = N_GROUPS * N_EXPERTS_PER_GROUP
MODEL_DEPTH = 4
DEEPNORM_ALPHA = (2 * MODEL_DEPTH) ** 0.25
LN_EPS = 1e-5
NEG_INF = -1e30
ATTN_SCALE = HEAD_DIM ** -0.5

ROUTER_LANES = 128
ATTN_TQ = 128
VMEM_LIMIT = 56 * 1024 * 1024

F32 = jnp.float32
BF16 = jnp.bfloat16


def _alibi_slopes():
    n = N_HEADS_A + N_HEADS_B
    s = (2.0 ** (-8.0 * np.arange(1, n + 1, dtype=np.float32) / n)).astype(np.float32)
    return s[0::2], s[1::2]


def _params(*sem):
    return pltpu.CompilerParams(dimension_semantics=sem, vmem_limit_bytes=VMEM_LIMIT)


def _layer_norm(x, g, b):
    mu = jnp.mean(x, axis=-1, keepdims=True)
    xc = x - mu
    var = jnp.mean(xc * xc, axis=-1, keepdims=True)
    return xc * lax.rsqrt(var + LN_EPS) * g + b


def _rms_norm(x, g):
    return x * lax.rsqrt(jnp.mean(x * x, axis=-1, keepdims=True) + LN_EPS) * g


def _ada_kernel(c_ref, w_ref, b_ref, o_ref):
    c = c_ref[...]
    a = (c * jax.nn.sigmoid(c)).astype(BF16)
    o_ref[...] = jnp.dot(a, w_ref[...].astype(BF16), preferred_element_type=F32) + b_ref[...]


def _ada_call(c_pad, w_ada, b_ada):
    depth, d, e = w_ada.shape
    nbp = c_pad.shape[0]
    tn = 1024
    return pl.pallas_call(
        _ada_kernel,
        grid=(depth, e // tn),
        in_specs=[
            pl.BlockSpec((nbp, d), lambda l, j: (0, 0)),
            pl.BlockSpec((None, d, tn), lambda l, j: (l, 0, j)),
            pl.BlockSpec((None, 1, tn), lambda l, j: (l, 0, j)),
        ],
        out_specs=pl.BlockSpec((None, nbp, tn), lambda l, j: (l, 0, j)),
        out_shape=jax.ShapeDtypeStruct((depth, nbp, e), F32),
        compiler_params=_params("arbitrary", "arbitrary"),
        name="ada_mod",
    )(c_pad, w_ada, b_ada.reshape(depth, 1, e))


def _ln_in_kernel(x_ref, g_ref, b_ref, o_ref):
    o_ref[...] = _layer_norm(x_ref[...], g_ref[...], b_ref[...])


def _ln_in_call(x, g, b):
    n, d = x.shape
    tm = 512
    return pl.pallas_call(
        _ln_in_kernel,
        grid=(n // tm,),
        in_specs=[
            pl.BlockSpec((tm, d), lambda i: (i, 0)),
            pl.BlockSpec((1, d), lambda i: (0, 0)),
            pl.BlockSpec((1, d), lambda i: (0, 0)),
        ],
        out_specs=pl.BlockSpec((tm, d), lambda i: (i, 0)),
        out_shape=jax.ShapeDtypeStruct((n, d), F32),
        compiler_params=_params("arbitrary"),
        name="ln_in",
    )(x, g.reshape(1, d), b.reshape(1, d))


class _Tokens:
    def __init__(self, bp, sp, bs, ss):
        self.bp, self.sp, self.bs, self.ss = bp, sp, bs, ss
        self.n_prompt = bp * sp
        self.n = bp * sp + bs * ss

    def batch_of_tile(self, i, tm):
        assert self.sp % tm == 0 and self.ss % tm == 0
        npt = self.n_prompt // tm
        return jnp.where(i < npt, i // (self.sp // tm), self.bp + (i - npt) // (self.ss // tm))


def _mod_spec(tok, tm, layer, k, d, n_grid):
    if n_grid == 1:
        return pl.BlockSpec((None, None, None, 1, d), lambda i: (layer, tok.batch_of_tile(i, tm), k, 0, 0))
    return pl.BlockSpec((None, None, None, 1, d), lambda i, j: (layer, tok.batch_of_tile(i, tm), k, 0, 0))


def _inproj_kernel(x_ref, sh_ref, sc_ref, w_ref, o_ref, h_scr):
    @pl.when(pl.program_id(1) == 0)
    def _():
        h_scr[...] = (x_ref[...] * (1.0 + sc_ref[...]) + sh_ref[...]).astype(BF16)

    o_ref[...] = jnp.dot(h_scr[...], w_ref[...], preferred_element_type=F32).astype(o_ref.dtype)


def _inproj_call(tok, layer, x, mod5, w_in_bf):
    n, d = x.shape
    tm, tn = 1024, 1536
    return pl.pallas_call(
        _inproj_kernel,
        grid=(n // tm, D_IN // tn),
        in_specs=[
            pl.BlockSpec((tm, d), lambda i, j: (i, 0)),
            _mod_spec(tok, tm, layer, 0, d, 2),
            _mod_spec(tok, tm, layer, 1, d, 2),
            pl.BlockSpec((None, d, tn), lambda i, j: (layer, 0, j)),
        ],
        out_specs=pl.BlockSpec((tm, tn), lambda i, j: (i, j)),
        out_shape=jax.ShapeDtypeStruct((n, D_IN), BF16),
        scratch_shapes=[pltpu.VMEM((tm, d), BF16)],
        compiler_params=_params("arbitrary", "arbitrary"),
        name=f"inproj_l{layer}",
    )(x, mod5, mod5, w_in_bf)


def _band_variant(blk, nblk):
    return jnp.where(blk == 0, 0, jnp.where(blk == nblk - 1, 2, 1))


def _fill_bias(bias_ref, base, slope, dil, radius, kw, offsets):
    row = lax.broadcasted_iota(jnp.int32, (ATTN_TQ, kw), 0)
    col = lax.broadcasted_iota(jnp.int32, (ATTN_TQ, kw), 1)
    for v, off in enumerate(offsets):
        a = jnp.abs(col - row + off)
        bias = -slope * (a * dil).astype(F32)
        bias_ref[base + v] = jnp.where(a <= radius, bias, NEG_INF)


def _scores(qb, kw, bias):
    s = lax.dot_general(qb, kw, (((1,), (1,)), ((), ())), preferred_element_type=F32)
    return s * ATTN_SCALE + bias


def _attn_a_kernel(slopes_ref, q_ref, k_ref, v_ref, o_ref, qf, kf, vf, m_s, l_s, b1, b2, b3, *, seq):
    tq = ATTN_TQ
    slope = slopes_ref[pl.program_id(1)]
    bias_refs = (b1, b2, b3)
    geom = []
    for (window, dil), bref in zip(DILATED_PATTERNS, bias_refs):
        radius = window // (2 * dil)
        length = seq // dil
        kw = min(tq + 2 * radius, length)
        nblk = length // tq
        _fill_bias(bref, 0, slope, dil, radius, kw, (0, -radius, tq - kw))
        geom.append((dil, radius, length, kw, nblk))

    def block(qb, kwin, vwin, bias):
        s = _scores(qb, kwin, bias)
        m = jnp.max(s, axis=-1, keepdims=True)
        p = jnp.exp(s - m)
        l = jnp.sum(p, axis=-1, keepdims=True)
        o = jnp.dot(p.astype(BF16), vwin, preferred_element_type=F32)
        return o, m, l

    dil, radius, length, kw, nblk = geom[0]
    assert dil == 1

    def body0(blk, carry):
        q0 = pl.multiple_of(blk * tq, tq)
        start = pl.multiple_of(jnp.clip(q0 - radius, 0, length - kw), 64)
        o, m, l = block(q_ref[pl.ds(q0, tq), :], k_ref[pl.ds(start, kw), :], v_ref[pl.ds(start, kw), :],
                        b1[_band_variant(blk, nblk)])
        o_ref[pl.ds(q0, tq), :] = o
        m_s[pl.ds(q0, tq), :] = jnp.broadcast_to(m, (tq, HEAD_DIM))
        l_s[pl.ds(q0, tq), :] = jnp.broadcast_to(l, (tq, HEAD_DIM))
        return carry

    lax.fori_loop(0, nblk, body0, 0)

    qf[...] = q_ref[...].astype(F32)
    kf[...] = k_ref[...].astype(F32)
    vf[...] = v_ref[...].astype(F32)

    for (dil, radius, length, kw, nblk), bref in zip(geom[1:], bias_refs[1:]):
        def body(i, carry, dil=dil, radius=radius, length=length, kw=kw, nblk=nblk, bref=bref):
            r = i // nblk
            blk = i % nblk
            q0 = blk * tq
            start = jnp.clip(q0 - radius, 0, length - kw)
            rows_q = pl.ds(r + dil * q0, tq, stride=dil)
            rows_k = pl.ds(r + dil * start, kw, stride=dil)
            o, m, l = block(qf[rows_q, :].astype(BF16), kf[rows_k, :].astype(BF16), vf[rows_k, :].astype(BF16),
                            bref[_band_variant(blk, nblk)])
            m_old = m_s[rows_q, :]
            m_new = jnp.maximum(m_old, m)
            a_old = jnp.exp(m_old - m_new)
            a_new = jnp.exp(m - m_new)
            o_ref[rows_q, :] = a_old * o_ref[rows_q, :] + a_new * o
            l_s[rows_q, :] = a_old * l_s[rows_q, :] + a_new * l
            m_s[rows_q, :] = m_new
            return carry

        lax.fori_loop(0, dil * nblk, body, 0)

    o_ref[...] = o_ref[...] / l_s[...]


def _attn_a_call(proj, slopes_a, seq, n_seq, row_block0):
    n = proj.shape[0]
    assert seq % (16 * ATTN_TQ) == 0
    n_variants = 3
    bias_shapes = []
    for window, dil in DILATED_PATTERNS:
        kw = min(ATTN_TQ + window // dil, seq // dil)
        bias_shapes.append(pltpu.VMEM((n_variants, ATTN_TQ, kw), F32))

    def spec(col0):
        return pl.BlockSpec((seq, HEAD_DIM), lambda b, h: (row_block0 + b, col0 + h))

    return pl.pallas_call(
        functools.partial(_attn_a_kernel, seq=seq),
        grid=(n_seq, N_HEADS_A),
        in_specs=[
            pl.BlockSpec(memory_space=pltpu.SMEM),
            spec(0), spec(N_HEADS_A), spec(2 * N_HEADS_A),
        ],
        out_specs=pl.BlockSpec((seq, HEAD_DIM), lambda b, h: (b, h)),
        out_shape=jax.ShapeDtypeStruct((n_seq * seq, D_A), F32),
        scratch_shapes=[pltpu.VMEM((seq, HEAD_DIM), F32) for _ in range(5)] + bias_shapes,
        compiler_params=_params("arbitrary", "arbitrary"),
        name=f"attn_a_s{seq}",
    )(slopes_a, proj, proj, proj)


def _attn_b_kernel(slopes_ref, sink_ref, q_ref, k_ref, v_ref, o_ref, bias, *, seq, layer):
    tq = ATTN_TQ
    kw = tq + 2 * SWA_RADIUS
    nblk = seq // tq
    g = pl.program_id(1)
    for hh in range(REP_B):
        _fill_bias(bias, 3 * hh, slopes_ref[g * REP_B + hh], 1, SWA_RADIUS, kw, (0, -SWA_RADIUS, tq - kw))

    def body(blk, carry):
        q0 = pl.multiple_of(blk * tq, tq)
        start = pl.multiple_of(jnp.clip(q0 - SWA_RADIUS, 0, seq - kw), tq)
        variant = _band_variant(blk, nblk)
        kwin = k_ref[pl.ds(start, kw), :]
        vwin = v_ref[pl.ds(start, kw), :]
        for hh in range(REP_B):
            cols = slice(hh * HEAD_DIM, (hh + 1) * HEAD_DIM)
            sink = sink_ref[layer, g * REP_B + hh]
            s = _scores(q_ref[pl.ds(q0, tq), cols], kwin, bias[3 * hh + variant])
            m = jnp.maximum(jnp.max(s, axis=-1, keepdims=True), sink)
            p = jnp.exp(s - m)
            l = jnp.sum(p, axis=-1, keepdims=True) + jnp.exp(sink - m)
            o = jnp.dot(p.astype(BF16), vwin, preferred_element_type=F32)
            o_ref[pl.ds(q0, tq), cols] = o / l
        return carry

    lax.fori_loop(0, nblk, body, 0)


def _attn_b_call(proj, slopes_b, sink_b, layer, seq, n_seq, row_block0):
    assert seq % ATTN_TQ == 0 and seq >= ATTN_TQ + 2 * SWA_RADIUS
    q_col0 = 3 * D_A // (REP_B * HEAD_DIM)
    k_col0 = (3 * D_A + D_B) // HEAD_DIM
    v_col0 = (3 * D_A + D_B + D_KV_B) // HEAD_DIM
    return pl.pallas_call(
        functools.partial(_attn_b_kernel, seq=seq, layer=layer),
        grid=(n_seq, N_KV_B),
        in_specs=[
            pl.BlockSpec(memory_space=pltpu.SMEM),
            pl.BlockSpec(memory_space=pltpu.SMEM),
            pl.BlockSpec((seq, REP_B * HEAD_DIM), lambda b, g: (row_block0 + b, q_col0 + g)),
            pl.BlockSpec((seq, HEAD_DIM), lambda b, g: (row_block0 + b, k_col0 + g)),
            pl.BlockSpec((seq, HEAD_DIM), lambda b, g: (row_block0 + b, v_col0 + g)),
        ],
        out_specs=pl.BlockSpec((seq, REP_B * HEAD_DIM), lambda b, g: (b, g)),
        out_shape=jax.ShapeDtypeStruct((n_seq * seq, D_B), F32),
        scratch_shapes=[pltpu.VMEM((3 * REP_B, ATTN_TQ, ATTN_TQ + 2 * SWA_RADIUS), F32)],
        compiler_params=_params("arbitrary", "arbitrary"),
        name=f"attn_b_s{seq}_l{layer}",
    )(slopes_b, sink_b, proj, proj, proj)


def _route(logits):
    lane = lax.broadcasted_iota(jnp.int32, logits.shape, 1).astype(F32)
    far = float(ROUTER_LANES)
    gl = jnp.where(lane < N_GROUPS, logits, NEG_INF)
    gmax = jnp.max(gl, axis=-1, keepdims=True)
    gsum = jnp.sum(jnp.exp(gl - gmax), axis=-1, keepdims=True)
    g_gate = 1.0 / gsum
    gidx = jnp.min(jnp.where(gl == gmax, lane, far), axis=-1, keepdims=True)
    base = N_GROUPS + N_EXPERTS_PER_GROUP * gidx
    el = jnp.where((lane >= base) & (lane < base + N_EXPERTS_PER_GROUP), logits, NEG_INF)
    m1 = jnp.max(el, axis=-1, keepdims=True)
    i1 = jnp.min(jnp.where(el == m1, lane, far), axis=-1, keepdims=True)
    el2 = jnp.where(lane == i1, NEG_INF, el)
    m2 = jnp.max(el2, axis=-1, keepdims=True)
    i2 = jnp.min(jnp.where(el2 == m2, lane, far), axis=-1, keepdims=True)
    e2 = jnp.exp(m2 - m1)
    w1 = 1.0 / (1.0 + e2)
    w2 = e2 / (1.0 + e2)
    return jnp.where(lane == i1, g_gate * w1, jnp.where(lane == i2, g_gate * w2, 0.0))


def _outproj_kernel(aa_ref, ab_ref, x_ref, wo_ref, gn_ref, g1_ref, sc2_ref, sh2_ref, lng_ref, lnb_ref,
                    wr_ref, br_ref, x1_ref, cw_ref):
    na = _rms_norm(aa_ref[...], gn_ref[:, :D_A]).astype(BF16)
    nb = _rms_norm(ab_ref[...], gn_ref[:, D_A:]).astype(BF16)
    mix = jnp.dot(na, wo_ref[:D_A, :], preferred_element_type=F32)
    mix = mix + jnp.dot(nb, wo_ref[D_A:, :], preferred_element_type=F32)
    y = DEEPNORM_ALPHA * x_ref[...] + (1.0 + g1_ref[...]) * mix
    x1 = _layer_norm(y, lng_ref[...], lnb_ref[...])
    x1_ref[...] = x1
    h2 = x1 * (1.0 + sc2_ref[...]) + sh2_ref[...]
    logits = jnp.dot(h2, wr_ref[...], preferred_element_type=F32, precision=lax.Precision.HIGHEST) + br_ref[...]
    cw_ref[...] = _route(logits)


def _outproj_call(tok, layer, attn_a, attn_b, x, mod5, w_out_bf, gn, ln_g, ln_b, w_router, b_router):
    n, d = x.shape
    tm = 512
    const = lambda shape: pl.BlockSpec(shape, lambda i: (layer,) + (0,) * (len(shape) - 1))
    return pl.pallas_call(
        _outproj_kernel,
        grid=(n // tm,),
        in_specs=[
            pl.BlockSpec((tm, D_A), lambda i: (i, 0)),
            pl.BlockSpec((tm, D_B), lambda i: (i, 0)),
            pl.BlockSpec((tm, d), lambda i: (i, 0)),
            const((None, D_A + D_B, d)),
            const((None, 1, D_A + D_B)),
            _mod_spec(tok, tm, layer, 2, d, 1),
            _mod_spec(tok, tm, layer, 4, d, 1),
            _mod_spec(tok, tm, layer, 3, d, 1),
            const((None, 1, d)),
            const((None, 1, d)),
            const((None, d, ROUTER_LANES)),
            const((None, 1, ROUTER_LANES)),
        ],
        out_specs=[
            pl.BlockSpec((tm, d), lambda i: (i, 0)),
            pl.BlockSpec((tm, ROUTER_LANES), lambda i: (i, 0)),
        ],
        out_shape=[
            jax.ShapeDtypeStruct((n, d), F32),
            jax.ShapeDtypeStruct((n, ROUTER_LANES), F32),
        ],
        compiler_params=_params("arbitrary"),
        name=f"outproj_l{layer}",
    )(attn_a, attn_b, x, w_out_bf, gn, mod5, mod5, mod5, ln_g, ln_b, w_router, b_router)


def _moe_kernel(x_ref, cw_ref, sc2_ref, sh2_ref, g2_ref, wg_ref, wu_ref, wd_ref, lng_ref, lnb_ref, o_ref,
                h_scr, acc):
    j = pl.program_id(1)

    @pl.when(j == 0)
    def _():
        h_scr[...] = (x_ref[...] * (1.0 + sc2_ref[...]) + sh2_ref[...]).astype(BF16)
        acc[...] = jnp.zeros_like(acc)

    h = h_scr[...]
    a = jnp.dot(h, wg_ref[...], preferred_element_type=F32)
    u = jnp.dot(h, wu_ref[...], preferred_element_type=F32)
    cw = cw_ref[...]
    lane = lax.broadcasted_iota(jnp.int32, cw.shape, 1)
    c = jnp.sum(jnp.where(lane == j + N_GROUPS, cw, 0.0), axis=-1, keepdims=True)
    hid = (a * jax.nn.sigmoid(a)) * u * c
    acc[...] += jnp.dot(hid.astype(BF16), wd_ref[...], preferred_element_type=F32)

    @pl.when(j == N_EXPERTS - 1)
    def _():
        y = DEEPNORM_ALPHA * x_ref[...] + (1.0 + g2_ref[...]) * acc[...]
        o_ref[...] = _layer_norm(y, lng_ref[...], lnb_ref[...])


def _moe_call(tok, layer, x1, cw, mod5, wg_bf, wu_bf, wd_bf, ln_g, ln_b):
    n, d = x1.shape
    f = wg_bf.shape[-1]
    tm = 512
    const = lambda shape: pl.BlockSpec(shape, lambda i, j: (layer,) + (0,) * (len(shape) - 1))
    return pl.pallas_call(
        _moe_kernel,
        grid=(n // tm, N_EXPERTS),
        in_specs=[
            pl.BlockSpec((tm, d), lambda i, j: (i, 0)),
            pl.BlockSpec((tm, ROUTER_LANES), lambda i, j: (i, 0)),
            _mod_spec(tok, tm, layer, 4, d, 2),
            _mod_spec(tok, tm, layer, 3, d, 2),
            _mod_spec(tok, tm, layer, 5, d, 2),
            pl.BlockSpec((None, None, d, f), lambda i, j: (layer, j, 0, 0)),
            pl.BlockSpec((None, None, d, f), lambda i, j: (layer, j, 0, 0)),
            pl.BlockSpec((None, None, f, d), lambda i, j: (layer, j, 0, 0)),
            const((None, 1, d)),
            const((None, 1, d)),
        ],
        out_specs=pl.BlockSpec((tm, d), lambda i, j: (i, 0)),
        out_shape=jax.ShapeDtypeStruct((n, d), F32),
        scratch_shapes=[pltpu.VMEM((tm, d), BF16), pltpu.VMEM((tm, d), F32)],
        compiler_params=_params("arbitrary", "arbitrary"),
        name=f"moe_l{layer}",
    )(x1, cw, mod5, mod5, mod5, wg_bf, wu_bf, wd_bf, ln_g, ln_b)


def kernel(x_prompt, x_sample, c_prompt, c_sample, ln_in_g, ln_in_b, w_ada, b_ada, w_in, sink_b, gn_a, gn_b,
           w_out, ln1_g, ln1_b, w_rg, b_rg, w_re, b_re, w_gate, w_up, w_down, ln2_g, ln2_b):
    bp, sp, d = x_prompt.shape
    bs, ss, _ = x_sample.shape
    depth = w_in.shape[0]
    tok = _Tokens(bp, sp, bs, ss)
    nb = bp + bs
    nbp = -(-nb // 8) * 8
    assert tok.n_prompt % ss == 0

    slopes_a, slopes_b = (jnp.asarray(s) for s in _alibi_slopes())
    w_in_bf = w_in.astype(BF16)
    w_out_bf = w_out.astype(BF16)
    g, e, f = N_GROUPS, N_EXPERTS_PER_GROUP, w_gate.shape[-1]
    wg_bf = w_gate.astype(BF16).reshape(depth, g * e, d, f)
    wu_bf = w_up.astype(BF16).reshape(depth, g * e, d, f)
    wd_bf = w_down.astype(BF16).reshape(depth, g * e, f, d)
    gn = jnp.concatenate([gn_a, gn_b], axis=-1).reshape(depth, 1, D_A + D_B)
    w_router = jnp.concatenate(
        [w_rg, jnp.transpose(w_re, (0, 2, 1, 3)).reshape(depth, d, g * e),
         jnp.zeros((depth, d, ROUTER_LANES - g - g * e), F32)], axis=-1)
    b_router = jnp.concatenate(
        [b_rg, b_re.reshape(depth, g * e), jnp.zeros((depth, ROUTER_LANES - g - g * e), F32)],
        axis=-1).reshape(depth, 1, ROUTER_LANES)
    ln1g, ln1b = ln1_g.reshape(depth, 1, d), ln1_b.reshape(depth, 1, d)
    ln2g, ln2b = ln2_g.reshape(depth, 1, d), ln2_b.reshape(depth, 1, d)

    c_pad = jnp.concatenate([c_prompt, c_sample, jnp.zeros((nbp - nb, d), F32)], axis=0)
    mod5 = _ada_call(c_pad, w_ada, b_ada).reshape(depth, nbp, 6, 1, d)

    x = jnp.concatenate([x_prompt.reshape(bp * sp, d), x_sample.reshape(bs * ss, d)], axis=0)
    x = _ln_in_call(x, ln_in_g, ln_in_b)

    for layer in range(depth):
        proj = _inproj_call(tok, layer, x, mod5, w_in_bf)
        aa = jnp.concatenate([
            _attn_a_call(proj, slopes_a, sp, bp, 0),
            _attn_a_call(proj, slopes_a, ss, bs, tok.n_prompt // ss)], axis=0)
        ab = jnp.concatenate([
            _attn_b_call(proj, slopes_b, sink_b, layer, sp, bp, 0),
            _attn_b_call(proj, slopes_b, sink_b, layer, ss, bs, tok.n_prompt // ss)], axis=0)
        x1, cw = _outproj_call(tok, layer, aa, ab, x, mod5, w_out_bf, gn, ln1g, ln1b, w_router, b_router)
        x = _moe_call(tok, layer, x1, cw, mod5, wg_bf, wu_bf, wd_bf, ln2g, ln2b)

    y_prompt = x[:tok.n_prompt].reshape(bp, sp, d)
    y_sample = x[tok.n_prompt:].reshape(bs, ss, d)
    return y_prompt, y_sample
```

```python
import functools
import math

import numpy as np
import jax
import jax.numpy as jnp
from jax import lax
from jax.experimental import pallas as pl
from jax.experimental.pallas import tpu as pltpu

HEAD_DIM = 128
N_HEADS_A = 8
N_HEADS_B = 8
N_KV_B = 2
REP_B = N_HEADS_B // N_KV_B
D_A = N_HEADS_A * HEAD_DIM
D_B = N_HEADS_B * HEAD_DIM
D_KV_B = N_KV_B * HEAD_DIM
D_IN = 3 * D_A + D_B + 2 * D_KV_B
DILATED_PATTERNS = ((128, 1), (512, 4), (2048, 16))
SWA_RADIUS = 128
N_GROUPS = 4
N_EXPERTS_PER_GROUP = 4
N_EXPERTS = N_GROUPS * N_EXPERTS_PER_GROUP
MODEL_DEPTH = 4
DEEPNORM_ALPHA = (2 * MODEL_DEPTH) ** 0.25
LN_EPS = 1e-5
NEG_INF = -1e30
ATTN_SCALE = HEAD_DIM ** -0.5

ROUTER_LANES = 128
ATTN_TQ = 128
ATTN_A_GROUP = 4
ATTN_B_GROUP = 1
VMEM_LIMIT = 56 * 1024 * 1024

F32 = jnp.float32
BF16 = jnp.bfloat16


def _alibi_slopes():
    n = N_HEADS_A + N_HEADS_B
    s = (2.0 ** (-8.0 * np.arange(1, n + 1, dtype=np.float32) / n)).astype(np.float32)
    return s[0::2], s[1::2]


def _params(*sem):
    return pltpu.CompilerParams(dimension_semantics=sem, vmem_limit_bytes=VMEM_LIMIT)


def _layer_norm(x, g, b):
    mu = jnp.mean(x, axis=-1, keepdims=True)
    xc = x - mu
    var = jnp.mean(xc * xc, axis=-1, keepdims=True)
    return xc * lax.rsqrt(var + LN_EPS) * g + b


def _rms_norm(x, g):
    return x * lax.rsqrt(jnp.mean(x * x, axis=-1, keepdims=True) + LN_EPS) * g


def _ada_kernel(c_ref, w_ref, b_ref, o_ref):
    c = c_ref[...]
    a = (c * jax.nn.sigmoid(c)).astype(BF16)
    o_ref[...] = jnp.dot(a, w_ref[...].astype(BF16), preferred_element_type=F32) + b_ref[...]


def _ada_call(c_pad, w_ada, b_ada):
    depth, d, e = w_ada.shape
    nbp = c_pad.shape[0]
    tn = 1024
    return pl.pallas_call(
        _ada_kernel,
        grid=(depth, e // tn),
        in_specs=[
            pl.BlockSpec((nbp, d), lambda l, j: (0, 0)),
            pl.BlockSpec((None, d, tn), lambda l, j: (l, 0, j)),
            pl.BlockSpec((None, 1, tn), lambda l, j: (l, 0, j)),
        ],
        out_specs=pl.BlockSpec((None, nbp, tn), lambda l, j: (l, 0, j)),
        out_shape=jax.ShapeDtypeStruct((depth, nbp, e), F32),
        compiler_params=_params("arbitrary", "arbitrary"),
        name="ada_mod",
    )(c_pad, w_ada, b_ada.reshape(depth, 1, e))


class _Tokens:
    def __init__(self, bp, sp, bs, ss):
        self.bp, self.sp, self.bs, self.ss = bp, sp, bs, ss
        self.n_prompt = bp * sp
        self.n_sample = bs * ss
        self.n = bp * sp + bs * ss

    def batch_of_tile(self, i, tm):
        assert self.sp % tm == 0 and self.ss % tm == 0
        npt = self.n_prompt // tm
        return jnp.where(i < npt, i // (self.sp // tm), self.bp + (i - npt) // (self.ss // tm))


def _mod_spec(tok, tm, layer, k, d, n_grid):
    if n_grid == 1:
        return pl.BlockSpec((None, None, None, 1, d), lambda i: (layer, tok.batch_of_tile(i, tm), k, 0, 0))
    return pl.BlockSpec((None, None, None, 1, d), lambda i, j: (layer, tok.batch_of_tile(i, tm), k, 0, 0))


def _ln_in_kernel(xp_ref, xs_ref, g_ref, b_ref, o_ref, *, n_prompt_tiles):
    i = pl.program_id(0)

    @pl.when(i < n_prompt_tiles)
    def _():
        o_ref[...] = _layer_norm(xp_ref[...], g_ref[...], b_ref[...])

    @pl.when(i >= n_prompt_tiles)
    def _():
        o_ref[...] = _layer_norm(xs_ref[...], g_ref[...], b_ref[...])


def _ln_in_call(tok, xp, xs, g, b):
    d = xp.shape[-1]
    tm = 512
    npt = tok.n_prompt // tm
    return pl.pallas_call(
        functools.partial(_ln_in_kernel, n_prompt_tiles=npt),
        grid=(tok.n // tm,),
        in_specs=[
            pl.BlockSpec((tm, d), lambda i: (jnp.minimum(i, npt - 1), 0)),
            pl.BlockSpec((tm, d), lambda i: (jnp.maximum(i - npt, 0), 0)),
            pl.BlockSpec((1, d), lambda i: (0, 0)),
            pl.BlockSpec((1, d), lambda i: (0, 0)),
        ],
        out_specs=pl.BlockSpec((tm, d), lambda i: (i, 0)),
        out_shape=jax.ShapeDtypeStruct((tok.n, d), F32),
        compiler_params=_params("arbitrary"),
        name="ln_in",
    )(xp, xs, g.reshape(1, d), b.reshape(1, d))


def _inproj_kernel(x_ref, sh_ref, sc_ref, w_ref, o_ref, h_scr):
    @pl.when(pl.program_id(1) == 0)
    def _():
        h_scr[...] = (x_ref[...] * (1.0 + sc_ref[...]) + sh_ref[...]).astype(BF16)

    o_ref[...] = jnp.dot(h_scr[...], w_ref[...], preferred_element_type=F32).astype(o_ref.dtype)


def _inproj_call(tok, layer, x, mod5, w_in_bf):
    n, d = x.shape
    tm, tn = 1024, 1536
    return pl.pallas_call(
        _inproj_kernel,
        grid=(n // tm, D_IN // tn),
        in_specs=[
            pl.BlockSpec((tm, d), lambda i, j: (i, 0)),
            _mod_spec(tok, tm, layer, 0, d, 2),
            _mod_spec(tok, tm, layer, 1, d, 2),
            pl.BlockSpec((None, d, tn), lambda i, j: (layer, 0, j)),
        ],
        out_specs=pl.BlockSpec((tm, tn), lambda i, j: (i, j)),
        out_shape=jax.ShapeDtypeStruct((n, D_IN), BF16),
        scratch_shapes=[pltpu.VMEM((tm, d), BF16)],
        compiler_params=_params("arbitrary", "arbitrary"),
        name=f"inproj_l{layer}",
    )(x, mod5, mod5, w_in_bf)


def _band_offsets(radius, kw):
    return (0, -radius, ATTN_TQ - kw)


def _fill_bias(bias_ref, base, slope, dil, radius, kw):
    row = lax.broadcasted_iota(jnp.int32, (ATTN_TQ, kw), 0)
    col = lax.broadcasted_iota(jnp.int32, (ATTN_TQ, kw), 1)
    for v, off in enumerate(_band_offsets(radius, kw)):
        a = jnp.abs(col - row + off)
        bias = -slope * (a * dil).astype(F32)
        bias_ref[base + v] = jnp.where(a <= radius, bias, NEG_INF)


def _band_attention(n_blocks, grp, nblk, kw, radius, heads, q_of, k_ref, v_ref, bias_ref, sink_of,
                    s_scr, p_scr, st_scr, finalize):
    tq = ATTN_TQ
    assert n_blocks % grp == 0
    n_it = n_blocks // grp
    offs = _band_offsets(radius, kw)
    align = math.gcd(tq, *[abs(o) for o in offs if o])

    def geom(it, j):
        i = it * grp + j
        variant = jnp.where(i % nblk == 0, 0, jnp.where(i % nblk == nblk - 1, 2, 1))
        off = jnp.where(variant == 0, offs[0], jnp.where(variant == 1, offs[1], offs[2]))
        q0 = pl.multiple_of(i * tq, tq)
        k0 = pl.multiple_of(i * tq + off, align)
        return i, variant, q0, k0

    def qk(it):
        it = jnp.asarray(it, jnp.int32)
        slot = it % 2
        for j in range(grp):
            _, variant, q0, k0 = geom(it, j)
            kwin = k_ref[pl.ds(k0, kw), :]
            for h in range(heads):
                s = lax.dot_general(q_of(h, pl.ds(q0, tq)), kwin, (((1,), (1,)), ((), ())),
                                    preferred_element_type=F32)
                s_scr[slot, j * heads + h, :, :kw] = s * ATTN_SCALE + bias_ref[3 * h + variant]

    def sm(it):
        it = jnp.asarray(it, jnp.int32)
        slot = it % 2
        for e in range(grp * heads):
            s = s_scr[slot, e, :, :kw]
            m = jnp.max(s, axis=-1, keepdims=True)
            sink = None if sink_of is None else sink_of(e % heads)
            if sink is not None:
                m = jnp.maximum(m, sink)
            p = jnp.exp(s - m)
            l = jnp.sum(p, axis=-1, keepdims=True)
            if sink is not None:
                l = l + jnp.exp(sink - m)
            p_scr[slot, e, :, :kw] = p.astype(BF16)
            st_scr[slot, e, 0] = jnp.broadcast_to(m, (tq, HEAD_DIM))
            st_scr[slot, e, 1] = jnp.broadcast_to(l, (tq, HEAD_DIM))

    def pv(it):
        it = jnp.asarray(it, jnp.int32)
        slot = it % 2
        for j in range(grp):
            i, _, _, k0 = geom(it, j)
            vwin = v_ref[pl.ds(k0, kw), :]
            for h in range(heads):
                e = j * heads + h
                o = jnp.dot(p_scr[slot, e, :, :kw], vwin, preferred_element_type=F32)
                finalize(i, h, o, st_scr[slot, e, 0], st_scr[slot, e, 1])

    qk(0)
    sm(0)
    if n_it > 1:
        qk(1)

        def body(it, carry):
            pv(it - 1)
            sm(it)
            qk(jnp.minimum(it + 1, n_it - 1))
            return carry

        lax.fori_loop(1, n_it, body, 0)
    pv(n_it - 1)


def _pipeline_scratch(entries, kw):
    return [pltpu.VMEM((2, entries, ATTN_TQ, kw), F32), pltpu.VMEM((2, entries, ATTN_TQ, kw), BF16),
            pltpu.VMEM((2, entries, 2, ATTN_TQ, HEAD_DIM), F32)]


def _attn_a_kernel(slopes_ref, buf_ref, q_ref, k_ref, v_ref, o_ref,
                   q4, k4, v4, q16, k16, v16, tmp_a, tmp_b, acc, m_s, l_s, b1, b2, b3,
                   s_scr, p_scr, st_scr, *, seq):
    del buf_ref
    tq = ATTN_TQ
    grp = ATTN_A_GROUP
    bias_refs = (b1, b2, b3)
    geom = []
    for window, dil in DILATED_PATTERNS:
        radius = window // (2 * dil)
        length = seq // dil
        geom.append((dil, radius, length, min(tq + 2 * radius, length), length // tq))
    assert [g[0] for g in geom] == [1, 4, 16]
    len4, len16 = seq // 4, seq // 16

    @pl.when(pl.program_id(1) == 0)
    def _():
        slope = slopes_ref[pl.program_id(0)]
        for (dil, radius, length, kw, nblk), bref in zip(geom, bias_refs):
            _fill_bias(bref, 0, slope, dil, radius, kw)

    for src, dst4, dst16 in ((q_ref, q4, q16), (k_ref, k4, k16), (v_ref, v4, v16)):
        tmp_a[...] = src[...].astype(F32)
        for r in range(4):
            cls = tmp_a[pl.ds(r, len4, stride=4), :]
            tmp_b[r * len4:(r + 1) * len4, :] = cls
            dst4[r * len4:(r + 1) * len4, :] = cls.astype(BF16)
        for r in range(16):
            dst16[r * len16:(r + 1) * len16, :] = tmp_b[pl.ds((r % 4) * len4 + r // 4, len16, stride=4), :].astype(BF16)

    def merge(rows, o, m, l):
        m_old = m_s[rows, :]
        m_new = jnp.maximum(m_old, m)
        a_old = jnp.exp(m_old - m_new)
        a_new = jnp.exp(m - m_new)
        acc[rows, :] = a_old * acc[rows, :] + a_new * o
        l_s[rows, :] = a_old * l_s[rows, :] + a_new * l
        m_s[rows, :] = m_new

    dil, radius, length, kw, nblk = geom[0]

    def fin0(i, h, o, m, l):
        rows = pl.ds(pl.multiple_of(i * tq, tq), tq)
        o_ref[rows, :] = o
        tmp_a[rows, :] = m
        tmp_b[rows, :] = l

    _band_attention(nblk, grp, nblk, kw, radius, 1, lambda h, rows: q_ref[rows, :], k_ref, v_ref, b1, None,
                    s_scr, p_scr, st_scr, fin0)
    for r in range(4):
        acc[r * len4:(r + 1) * len4, :] = o_ref[pl.ds(r, len4, stride=4), :]
        m_s[r * len4:(r + 1) * len4, :] = tmp_a[pl.ds(r, len4, stride=4), :]
        l_s[r * len4:(r + 1) * len4, :] = tmp_b[pl.ds(r, len4, stride=4), :]

    dil, radius, length, kw, nblk = geom[1]

    def fin4(i, h, o, m, l):
        merge(pl.ds(pl.multiple_of(i * tq, tq), tq), o, m, l)

    _band_attention(seq // tq, grp, nblk, kw, radius, 1, lambda h, rows: q4[rows, :], k4, v4, b2, None,
                    s_scr, p_scr, st_scr, fin4)

    dil, radius, length, kw, nblk = geom[2]

    def fin16(i, h, o, m, l):
        r16 = i // nblk
        l0 = (i % nblk) * tq
        merge(pl.ds((r16 % 4) * len4 + 4 * l0 + r16 // 4, tq, stride=4), o, m, l)

    _band_attention(seq // tq, grp, nblk, kw, radius, 1, lambda h, rows: q16[rows, :], k16, v16, b3, None,
                    s_scr, p_scr, st_scr, fin16)

    for r in range(4):
        o_ref[pl.ds(r, len4, stride=4), :] = acc[r * len4:(r + 1) * len4, :] / l_s[r * len4:(r + 1) * len4, :]


def _attn_a_call(buf, proj, slopes_a, seq, n_seq, row_block0):
    n = proj.shape[0]
    assert seq % (16 * ATTN_TQ) == 0
    bias_shapes = []
    kw_max = 0
    for window, dil in DILATED_PATTERNS:
        kw = min(ATTN_TQ + window // dil, seq // dil)
        kw_max = max(kw_max, kw)
        bias_shapes.append(pltpu.VMEM((3, ATTN_TQ, kw), F32))

    def spec(col0):
        return pl.BlockSpec((seq, HEAD_DIM), lambda h, b: (row_block0 + b, col0 + h))

    if buf is None:
        buf = jnp.zeros((8, 128), F32)
        aliases = {}
    else:
        aliases = {1: 0}
    rows_bf = pltpu.VMEM((seq, HEAD_DIM), BF16)
    rows_f32 = pltpu.VMEM((seq, HEAD_DIM), F32)
    return pl.pallas_call(
        functools.partial(_attn_a_kernel, seq=seq),
        grid=(N_HEADS_A, n_seq),
        in_specs=[
            pl.BlockSpec(memory_space=pltpu.SMEM),
            pl.BlockSpec(memory_space=pl.ANY),
            spec(0), spec(N_HEADS_A), spec(2 * N_HEADS_A),
        ],
        out_specs=pl.BlockSpec((seq, HEAD_DIM), lambda h, b: (row_block0 + b, h)),
        out_shape=jax.ShapeDtypeStruct((n, D_A + D_B), F32),
        input_output_aliases=aliases,
        scratch_shapes=[rows_bf] * 6 + [rows_f32] * 5 + bias_shapes + _pipeline_scratch(ATTN_A_GROUP, kw_max),
        compiler_params=_params("arbitrary", "arbitrary"),
        name=f"attn_a_s{seq}",
    )(slopes_a, buf, proj, proj, proj)


def _attn_b_kernel(slopes_ref, sink_ref, buf_ref, q_ref, k_ref, v_ref, o_ref, bias, s_scr, p_scr, st_scr,
                   *, seq, layer):
    del buf_ref
    tq = ATTN_TQ
    kw = tq + 2 * SWA_RADIUS
    nblk = seq // tq
    g = pl.program_id(0)

    @pl.when(pl.program_id(1) == 0)
    def _():
        for hh in range(REP_B):
            _fill_bias(bias, 3 * hh, slopes_ref[g * REP_B + hh], 1, SWA_RADIUS, kw)

    def fin(i, h, o, m, l):
        o_ref[pl.ds(pl.multiple_of(i * tq, tq), tq), h * HEAD_DIM:(h + 1) * HEAD_DIM] = o / l

    _band_attention(nblk, ATTN_B_GROUP, nblk, kw, SWA_RADIUS, REP_B,
                    lambda h, rows: q_ref[rows, h * HEAD_DIM:(h + 1) * HEAD_DIM], k_ref, v_ref, bias,
                    lambda h: sink_ref[layer, g * REP_B + h], s_scr, p_scr, st_scr, fin)


def _attn_b_call(buf, proj, slopes_b, sink_b, layer, seq, n_seq, row_block0):
    n = proj.shape[0]
    kw = ATTN_TQ + 2 * SWA_RADIUS
    assert seq % ATTN_TQ == 0 and seq >= kw
    wq = REP_B * HEAD_DIM
    q_col0 = 3 * D_A // wq
    k_col0 = (3 * D_A + D_B) // HEAD_DIM
    v_col0 = (3 * D_A + D_B + D_KV_B) // HEAD_DIM
    return pl.pallas_call(
        functools.partial(_attn_b_kernel, seq=seq, layer=layer),
        grid=(N_KV_B, n_seq),
        in_specs=[
            pl.BlockSpec(memory_space=pltpu.SMEM),
            pl.BlockSpec(memory_space=pltpu.SMEM),
            pl.BlockSpec(memory_space=pl.ANY),
            pl.BlockSpec((seq, wq), lambda g, b: (row_block0 + b, q_col0 + g)),
            pl.BlockSpec((seq, HEAD_DIM), lambda g, b: (row_block0 + b, k_col0 + g)),
            pl.BlockSpec((seq, HEAD_DIM), lambda g, b: (row_block0 + b, v_col0 + g)),
        ],
        out_specs=pl.BlockSpec((seq, wq), lambda g, b: (row_block0 + b, D_A // wq + g)),
        out_shape=jax.ShapeDtypeStruct((n, D_A + D_B), F32),
        input_output_aliases={2: 0},
        scratch_shapes=[pltpu.VMEM((3 * REP_B, ATTN_TQ, kw), F32)] + _pipeline_scratch(ATTN_B_GROUP * REP_B, kw),
        compiler_params=_params("arbitrary", "arbitrary"),
        name=f"attn_b_s{seq}_l{layer}",
    )(slopes_b, sink_b, buf, proj, proj, proj)


def _route(logits):
    lane = lax.broadcasted_iota(jnp.int32, logits.shape, 1).astype(F32)
    far = float(ROUTER_LANES)
    gl = jnp.where(lane < N_GROUPS, logits, NEG_INF)
    gmax = jnp.max(gl, axis=-1, keepdims=True)
    gsum = jnp.sum(jnp.exp(gl - gmax), axis=-1, keepdims=True)
    g_gate = 1.0 / gsum
    gidx = jnp.min(jnp.where(gl == gmax, lane, far), axis=-1, keepdims=True)
    base = N_GROUPS + N_EXPERTS_PER_GROUP * gidx
    el = jnp.where((lane >= base) & (lane < base + N_EXPERTS_PER_GROUP), logits, NEG_INF)
    m1 = jnp.max(el, axis=-1, keepdims=True)
    i1 = jnp.min(jnp.where(el == m1, lane, far), axis=-1, keepdims=True)
    el2 = jnp.where(lane == i1, NEG_INF, el)
    m2 = jnp.max(el2, axis=-1, keepdims=True)
    i2 = jnp.min(jnp.where(el2 == m2, lane, far), axis=-1, keepdims=True)
    e2 = jnp.exp(m2 - m1)
    w1 = 1.0 / (1.0 + e2)
    w2 = e2 / (1.0 + e2)
    return jnp.where(lane == i1, g_gate * w1, jnp.where(lane == i2, g_gate * w2, 0.0))


def _outproj_kernel(a_ref, x_ref, wo_ref, gn_ref, g1_ref, sc2_ref, sh2_ref, lng_ref, lnb_ref,
                    wr_ref, br_ref, x1_ref, cw_ref):
    na = _rms_norm(a_ref[:, :D_A], gn_ref[:, :D_A]).astype(BF16)
    nb = _rms_norm(a_ref[:, D_A:], gn_ref[:, D_A:]).astype(BF16)
    mix = jnp.dot(na, wo_ref[:D_A, :], preferred_element_type=F32)
    mix = mix + jnp.dot(nb, wo_ref[D_A:, :], preferred_element_type=F32)
    y = DEEPNORM_ALPHA * x_ref[...] + (1.0 + g1_ref[...]) * mix
    x1 = _layer_norm(y, lng_ref[...], lnb_ref[...])
    x1_ref[...] = x1
    h2 = (x1 * (1.0 + sc2_ref[...]) + sh2_ref[...]).astype(BF16)
    logits = jnp.dot(h2, wr_ref[...], preferred_element_type=F32) + br_ref[...]
    cw_ref[...] = _route(logits)


def _outproj_call(tok, layer, attn, x, mod5, w_out_bf, gn, ln_g, ln_b, w_router_bf, b_router):
    n, d = x.shape
    tm = 512
    const = lambda shape: pl.BlockSpec(shape, lambda i: (layer,) + (0,) * (len(shape) - 1))
    return pl.pallas_call(
        _outproj_kernel,
        grid=(n // tm,),
        in_specs=[
            pl.BlockSpec((tm, D_A + D_B), lambda i: (i, 0)),
            pl.BlockSpec((tm, d), lambda i: (i, 0)),
            const((None, D_A + D_B, d)),
            const((None, 1, D_A + D_B)),
            _mod_spec(tok, tm, layer, 2, d, 1),
            _mod_spec(tok, tm, layer, 4, d, 1),
            _mod_spec(tok, tm, layer, 3, d, 1),
            const((None, 1, d)),
            const((None, 1, d)),
            const((None, d, ROUTER_LANES)),
            const((None, 1, ROUTER_LANES)),
        ],
        out_specs=[
            pl.BlockSpec((tm, d), lambda i: (i, 0)),
            pl.BlockSpec((tm, ROUTER_LANES), lambda i: (i, 0)),
        ],
        out_shape=[
            jax.ShapeDtypeStruct((n, d), F32),
            jax.ShapeDtypeStruct((n, ROUTER_LANES), F32),
        ],
        compiler_params=_params("arbitrary"),
        name=f"outproj_l{layer}",
    )(attn, x, w_out_bf, gn, mod5, mod5, mod5, ln_g, ln_b, w_router_bf, b_router)


def _moe_kernel(x_ref, cw_ref, sc2_ref, sh2_ref, g2_ref, wg_ref, wu_ref, wd_ref, lng_ref, lnb_ref, *rest,
                n_prompt_tiles):
    if n_prompt_tiles is None:
        o_ref, h_scr, acc = rest
    else:
        op_ref, os_ref, h_scr, acc = rest
    i = pl.program_id(0)
    j = pl.program_id(1)

    @pl.when(j == 0)
    def _():
        h_scr[...] = (x_ref[...] * (1.0 + sc2_ref[...]) + sh2_ref[...]).astype(BF16)
        acc[...] = jnp.zeros_like(acc)

    h = h_scr[...]
    a = jnp.dot(h, wg_ref[...], preferred_element_type=F32)
    u = jnp.dot(h, wu_ref[...], preferred_element_type=F32)
    cw = cw_ref[...]
    lane = lax.broadcasted_iota(jnp.int32, cw.shape, 1)
    c = jnp.sum(jnp.where(lane == j + N_GROUPS, cw, 0.0), axis=-1, keepdims=True)
    hid = (a * jax.nn.sigmoid(a)) * u * c
    acc[...] += jnp.dot(hid.astype(BF16), wd_ref[...], preferred_element_type=F32)

    def finish():
        y = DEEPNORM_ALPHA * x_ref[...] + (1.0 + g2_ref[...]) * acc[...]
        return _layer_norm(y, lng_ref[...], lnb_ref[...])

    last = j == N_EXPERTS - 1
    if n_prompt_tiles is None:
        @pl.when(last)
        def _():
            o_ref[...] = finish()
    else:
        @pl.when(last & (i < n_prompt_tiles))
        def _():
            op_ref[...] = finish()

        @pl.when(last & (i >= n_prompt_tiles))
        def _():
            os_ref[...] = finish()


def _moe_call(tok, layer, final, x1, cw, mod5, wg_bf, wu_bf, wd_bf, ln_g, ln_b):
    n, d = x1.shape
    f = wg_bf.shape[-1]
    tm = 512
    npt = tok.n_prompt // tm
    const = lambda shape: pl.BlockSpec(shape, lambda i, j: (layer,) + (0,) * (len(shape) - 1))
    if final:
        out_specs = [
            pl.BlockSpec((tm, d), lambda i, j: (jnp.minimum(i, npt - 1), 0)),
            pl.BlockSpec((tm, d), lambda i, j: (jnp.maximum(i - npt, 0), 0)),
        ]
        out_shape = [jax.ShapeDtypeStruct((tok.n_prompt, d), F32), jax.ShapeDtypeStruct((tok.n_sample, d), F32)]
    else:
        out_specs = pl.BlockSpec((tm, d), lambda i, j: (i, 0))
        out_shape = jax.ShapeDtypeStruct((n, d), F32)
    return pl.pallas_call(
        functools.partial(_moe_kernel, n_prompt_tiles=npt if final else None),
        grid=(n // tm, N_EXPERTS),
        in_specs=[
            pl.BlockSpec((tm, d), lambda i, j: (i, 0)),
            pl.BlockSpec((tm, ROUTER_LANES), lambda i, j: (i, 0)),
            _mod_spec(tok, tm, layer, 4, d, 2),
            _mod_spec(tok, tm, layer, 3, d, 2),
            _mod_spec(tok, tm, layer, 5, d, 2),
            pl.BlockSpec((None, None, d, f), lambda i, j: (layer, j, 0, 0)),
            pl.BlockSpec((None, None, d, f), lambda i, j: (layer, j, 0, 0)),
            pl.BlockSpec((None, None, f, d), lambda i, j: (layer, j, 0, 0)),
            const((None, 1, d)),
            const((None, 1, d)),
        ],
        out_specs=out_specs,
        out_shape=out_shape,
        scratch_shapes=[pltpu.VMEM((tm, d), BF16), pltpu.VMEM((tm, d), F32)],
        compiler_params=_params("arbitrary", "arbitrary"),
        name=f"moe_l{layer}",
    )(x1, cw, mod5, mod5, mod5, wg_bf, wu_bf, wd_bf, ln_g, ln_b)


def kernel(x_prompt, x_sample, c_prompt, c_sample, ln_in_g, ln_in_b, w_ada, b_ada, w_in, sink_b, gn_a, gn_b,
           w_out, ln1_g, ln1_b, w_rg, b_rg, w_re, b_re, w_gate, w_up, w_down, ln2_g, ln2_b):
    bp, sp, d = x_prompt.shape
    bs, ss, _ = x_sample.shape
    depth = w_in.shape[0]
    tok = _Tokens(bp, sp, bs, ss)
    nb = bp + bs
    nbp = -(-nb // 8) * 8
    assert tok.n_prompt % ss == 0

    slopes_a, slopes_b = (jnp.asarray(s) for s in _alibi_slopes())
    w_in_bf = w_in.astype(BF16)
    w_out_bf = w_out.astype(BF16)
    g, e, f = N_GROUPS, N_EXPERTS_PER_GROUP, w_gate.shape[-1]
    wg_bf = w_gate.astype(BF16).reshape(depth, g * e, d, f)
    wu_bf = w_up.astype(BF16).reshape(depth, g * e, d, f)
    wd_bf = w_down.astype(BF16).reshape(depth, g * e, f, d)
    gn = jnp.concatenate([gn_a, gn_b], axis=-1).reshape(depth, 1, D_A + D_B)
    w_router_bf = jnp.concatenate(
        [w_rg, jnp.transpose(w_re, (0, 2, 1, 3)).reshape(depth, d, g * e),
         jnp.zeros((depth, d, ROUTER_LANES - g - g * e), F32)], axis=-1).astype(BF16)
    b_router = jnp.concatenate(
        [b_rg, b_re.reshape(depth, g * e), jnp.zeros((depth, ROUTER_LANES - g - g * e), F32)],
        axis=-1).reshape(depth, 1, ROUTER_LANES)
    ln1g, ln1b = ln1_g.reshape(depth, 1, d), ln1_b.reshape(depth, 1, d)
    ln2g, ln2b = ln2_g.reshape(depth, 1, d), ln2_b.reshape(depth, 1, d)

    c_pad = jnp.concatenate([c_prompt, c_sample, jnp.zeros((nbp - nb, d), F32)], axis=0)
    mod5 = _ada_call(c_pad, w_ada, b_ada).reshape(depth, nbp, 6, 1, d)

    x = _ln_in_call(tok, x_prompt.reshape(bp * sp, d), x_sample.reshape(bs * ss, d), ln_in_g, ln_in_b)

    sample_block0 = tok.n_prompt // ss
    for layer in range(depth):
        proj = _inproj_call(tok, layer, x, mod5, w_in_bf)
        attn = _attn_a_call(None, proj, slopes_a, sp, bp, 0)
        attn = _attn_a_call(attn, proj, slopes_a, ss, bs, sample_block0)
        attn = _attn_b_call(attn, proj, slopes_b, sink_b, layer, sp, bp, 0)
        attn = _attn_b_call(attn, proj, slopes_b, sink_b, layer, ss, bs, sample_block0)
        x1, cw = _outproj_call(tok, layer, attn, x, mod5, w_out_bf, gn, ln1g, ln1b, w_router_bf, b_router)
        x = _moe_call(tok, layer, layer == depth - 1, x1, cw, mod5, wg_bf, wu_bf, wd_bf, ln2g, ln2b)

    y_prompt, y_sample = x
    return y_prompt.reshape(bp, sp, d), y_sample.reshape(bs, ss, d)
```

```python
import functools
import math

import numpy as np
import jax
import jax.numpy as jnp
from jax import lax
from jax.experimental import pallas as pl
from jax.experimental.pallas import tpu as pltpu

HEAD_DIM = 128
N_HEADS_A = 8
N_HEADS_B = 8
N_KV_B = 2
REP_B = N_HEADS_B // N_KV_B
D_A = N_HEADS_A * HEAD_DIM
D_B = N_HEADS_B * HEAD_DIM
D_KV_B = N_KV_B * HEAD_DIM
D_IN = 3 * D_A + D_B + 2 * D_KV_B
DILATED_PATTERNS = ((128, 1), (512, 4), (2048, 16))
SWA_RADIUS = 128
N_GROUPS = 4
N_EXPERTS_PER_GROUP = 4
N_EXPERTS = N_GROUPS * N_EXPERTS_PER_GROUP
MODEL_DEPTH = 4
DEEPNORM_ALPHA = (2 * MODEL_DEPTH) ** 0.25
LN_EPS = 1e-5
NEG_INF = -1e30
ATTN_SCALE = HEAD_DIM ** -0.5

ROUTER_LANES = 128
ROUTE_W0 = 4
MOE_TM = 512
DMA_ISSUE_UNROLL = 8
ATTN_TQ = 128
ATTN_A_GROUP = 4
ATTN_B_GROUP = 1
VMEM_LIMIT = 56 * 1024 * 1024

F32 = jnp.float32
BF16 = jnp.bfloat16


def _alibi_slopes():
    n = N_HEADS_A + N_HEADS_B
    s = (2.0 ** (-8.0 * np.arange(1, n + 1, dtype=np.float32) / n)).astype(np.float32)
    return s[0::2], s[1::2]


def _params(*sem):
    return pltpu.CompilerParams(dimension_semantics=sem, vmem_limit_bytes=VMEM_LIMIT)


def _layer_norm(x, g, b):
    mu = jnp.mean(x, axis=-1, keepdims=True)
    xc = x - mu
    var = jnp.mean(xc * xc, axis=-1, keepdims=True)
    return xc * lax.rsqrt(var + LN_EPS) * g + b


def _rms_norm(x, g):
    return x * lax.rsqrt(jnp.mean(x * x, axis=-1, keepdims=True) + LN_EPS) * g


def _ada_kernel(c_ref, w_ref, b_ref, o_ref):
    c = c_ref[...]
    a = (c * jax.nn.sigmoid(c)).astype(BF16)
    o_ref[...] = jnp.dot(a, w_ref[...].astype(BF16), preferred_element_type=F32) + b_ref[...]


def _ada_call(c_pad, w_ada, b_ada):
    depth, d, e = w_ada.shape
    nbp = c_pad.shape[0]
    tn = 1024
    return pl.pallas_call(
        _ada_kernel,
        grid=(depth, e // tn),
        in_specs=[
            pl.BlockSpec((nbp, d), lambda l, j: (0, 0)),
            pl.BlockSpec((None, d, tn), lambda l, j: (l, 0, j)),
            pl.BlockSpec((None, 1, tn), lambda l, j: (l, 0, j)),
        ],
        out_specs=pl.BlockSpec((None, nbp, tn), lambda l, j: (l, 0, j)),
        out_shape=jax.ShapeDtypeStruct((depth, nbp, e), F32),
        compiler_params=_params("arbitrary", "arbitrary"),
        name="ada_mod",
    )(c_pad, w_ada, b_ada.reshape(depth, 1, e))


class _Tokens:
    def __init__(self, bp, sp, bs, ss):
        self.bp, self.sp, self.bs, self.ss = bp, sp, bs, ss
        self.n_prompt = bp * sp
        self.n_sample = bs * ss
        self.n = bp * sp + bs * ss

    def batch_of_tile(self, i, tm):
        assert self.sp % tm == 0 and self.ss % tm == 0
        npt = self.n_prompt // tm
        return jnp.where(i < npt, i // (self.sp // tm), self.bp + (i - npt) // (self.ss // tm))


def _mod_spec(tok, tm, layer, k, d, n_grid):
    if n_grid == 1:
        return pl.BlockSpec((None, None, None, 1, d), lambda i: (layer, tok.batch_of_tile(i, tm), k, 0, 0))
    return pl.BlockSpec((None, None, None, 1, d), lambda i, j: (layer, tok.batch_of_tile(i, tm), k, 0, 0))


def _ln_in_kernel(xp_ref, xs_ref, g_ref, b_ref, o_ref, *, n_prompt_tiles):
    i = pl.program_id(0)

    @pl.when(i < n_prompt_tiles)
    def _():
        o_ref[...] = _layer_norm(xp_ref[...], g_ref[...], b_ref[...])

    @pl.when(i >= n_prompt_tiles)
    def _():
        o_ref[...] = _layer_norm(xs_ref[...], g_ref[...], b_ref[...])


def _ln_in_call(tok, xp, xs, g, b):
    d = xp.shape[-1]
    tm = 512
    npt = tok.n_prompt // tm
    return pl.pallas_call(
        functools.partial(_ln_in_kernel, n_prompt_tiles=npt),
        grid=(tok.n // tm,),
        in_specs=[
            pl.BlockSpec((tm, d), lambda i: (jnp.minimum(i, npt - 1), 0)),
            pl.BlockSpec((tm, d), lambda i: (jnp.maximum(i - npt, 0), 0)),
            pl.BlockSpec((1, d), lambda i: (0, 0)),
            pl.BlockSpec((1, d), lambda i: (0, 0)),
        ],
        out_specs=pl.BlockSpec((tm, d), lambda i: (i, 0)),
        out_shape=jax.ShapeDtypeStruct((tok.n, d), F32),
        compiler_params=_params("arbitrary"),
        name="ln_in",
    )(xp, xs, g.reshape(1, d), b.reshape(1, d))


def _inproj_kernel(x_ref, sh_ref, sc_ref, w_ref, o_ref, h_scr):
    @pl.when(pl.program_id(1) == 0)
    def _():
        h_scr[...] = (x_ref[...] * (1.0 + sc_ref[...]) + sh_ref[...]).astype(BF16)

    o_ref[...] = jnp.dot(h_scr[...], w_ref[...], preferred_element_type=F32).astype(o_ref.dtype)


def _inproj_call(tok, layer, x, mod5, w_in_bf):
    n, d = x.shape
    tm, tn = 1024, 1536
    return pl.pallas_call(
        _inproj_kernel,
        grid=(n // tm, D_IN // tn),
        in_specs=[
            pl.BlockSpec((tm, d), lambda i, j: (i, 0)),
            _mod_spec(tok, tm, layer, 0, d, 2),
            _mod_spec(tok, tm, layer, 1, d, 2),
            pl.BlockSpec((None, d, tn), lambda i, j: (layer, 0, j)),
        ],
        out_specs=pl.BlockSpec((tm, tn), lambda i, j: (i, j)),
        out_shape=jax.ShapeDtypeStruct((n, D_IN), BF16),
        scratch_shapes=[pltpu.VMEM((tm, d), BF16)],
        compiler_params=_params("arbitrary", "arbitrary"),
        name=f"inproj_l{layer}",
    )(x, mod5, mod5, w_in_bf)


def _band_offsets(radius, kw):
    return (0, -radius, ATTN_TQ - kw)


def _fill_bias(bias_ref, base, slope, dil, radius, kw):
    row = lax.broadcasted_iota(jnp.int32, (ATTN_TQ, kw), 0)
    col = lax.broadcasted_iota(jnp.int32, (ATTN_TQ, kw), 1)
    for v, off in enumerate(_band_offsets(radius, kw)):
        a = jnp.abs(col - row + off)
        bias = -slope * (a * dil).astype(F32)
        bias_ref[base + v] = jnp.where(a <= radius, bias, NEG_INF)


def _band_attention(n_blocks, grp, nblk, kw, radius, heads, q_of, k_ref, v_ref, bias_ref, sink_of,
                    s_scr, p_scr, st_scr, finalize):
    tq = ATTN_TQ
    assert n_blocks % grp == 0
    n_it = n_blocks // grp
    offs = _band_offsets(radius, kw)
    align = math.gcd(tq, *[abs(o) for o in offs if o])

    def geom(it, j):
        i = it * grp + j
        variant = jnp.where(i % nblk == 0, 0, jnp.where(i % nblk == nblk - 1, 2, 1))
        off = jnp.where(variant == 0, offs[0], jnp.where(variant == 1, offs[1], offs[2]))
        q0 = pl.multiple_of(i * tq, tq)
        k0 = pl.multiple_of(i * tq + off, align)
        return i, variant, q0, k0

    def qk(it):
        it = jnp.asarray(it, jnp.int32)
        slot = it % 2
        for j in range(grp):
            _, variant, q0, k0 = geom(it, j)
            kwin = k_ref[pl.ds(k0, kw), :]
            for h in range(heads):
                s = lax.dot_general(q_of(h, pl.ds(q0, tq)), kwin, (((1,), (1,)), ((), ())),
                                    preferred_element_type=F32)
                s_scr[slot, j * heads + h, :, :kw] = s * ATTN_SCALE + bias_ref[3 * h + variant]

    def sm(it):
        it = jnp.asarray(it, jnp.int32)
        slot = it % 2
        for e in range(grp * heads):
            s = s_scr[slot, e, :, :kw]
            m = jnp.max(s, axis=-1, keepdims=True)
            sink = None if sink_of is None else sink_of(e % heads)
            if sink is not None:
                m = jnp.maximum(m, sink)
            p = jnp.exp(s - m)
            l = jnp.sum(p, axis=-1, keepdims=True)
            if sink is not None:
                l = l + jnp.exp(sink - m)
            p_scr[slot, e, :, :kw] = p.astype(BF16)
            st_scr[slot, e, 0] = jnp.broadcast_to(m, (tq, HEAD_DIM))
            st_scr[slot, e, 1] = jnp.broadcast_to(l, (tq, HEAD_DIM))

    def pv(it):
        it = jnp.asarray(it, jnp.int32)
        slot = it % 2
        for j in range(grp):
            i, _, _, k0 = geom(it, j)
            vwin = v_ref[pl.ds(k0, kw), :]
            for h in range(heads):
                e = j * heads + h
                o = jnp.dot(p_scr[slot, e, :, :kw], vwin, preferred_element_type=F32)
                finalize(i, h, o, st_scr[slot, e, 0], st_scr[slot, e, 1])

    qk(0)
    sm(0)
    if n_it > 1:
        qk(1)

        def body(it, carry):
            pv(it - 1)
            sm(it)
            qk(jnp.minimum(it + 1, n_it - 1))
            return carry

        lax.fori_loop(1, n_it, body, 0)
    pv(n_it - 1)


def _pipeline_scratch(entries, kw):
    return [pltpu.VMEM((2, entries, ATTN_TQ, kw), F32), pltpu.VMEM((2, entries, ATTN_TQ, kw), BF16),
            pltpu.VMEM((2, entries, 2, ATTN_TQ, HEAD_DIM), F32)]


def _attn_a_kernel(slopes_ref, buf_ref, q_ref, k_ref, v_ref, o_ref,
                   q4, k4, v4, q16, k16, v16, tmp_a, tmp_b, acc, m_s, l_s, b1, b2, b3,
                   s_scr, p_scr, st_scr, *, seq):
    del buf_ref
    tq = ATTN_TQ
    grp = ATTN_A_GROUP
    bias_refs = (b1, b2, b3)
    geom = []
    for window, dil in DILATED_PATTERNS:
        radius = window // (2 * dil)
        length = seq // dil
        geom.append((dil, radius, length, min(tq + 2 * radius, length), length // tq))
    assert [g[0] for g in geom] == [1, 4, 16]
    len4, len16 = seq // 4, seq // 16

    @pl.when(pl.program_id(1) == 0)
    def _():
        slope = slopes_ref[pl.program_id(0)]
        for (dil, radius, length, kw, nblk), bref in zip(geom, bias_refs):
            _fill_bias(bref, 0, slope, dil, radius, kw)

    for src, dst4, dst16 in ((q_ref, q4, q16), (k_ref, k4, k16), (v_ref, v4, v16)):
        tmp_a[...] = src[...].astype(F32)
        for r in range(4):
            cls = tmp_a[pl.ds(r, len4, stride=4), :]
            tmp_b[r * len4:(r + 1) * len4, :] = cls
            dst4[r * len4:(r + 1) * len4, :] = cls.astype(BF16)
        for r in range(16):
            dst16[r * len16:(r + 1) * len16, :] = tmp_b[pl.ds((r % 4) * len4 + r // 4, len16, stride=4), :].astype(BF16)

    def merge(rows, o, m, l):
        m_old = m_s[rows, :]
        m_new = jnp.maximum(m_old, m)
        a_old = jnp.exp(m_old - m_new)
        a_new = jnp.exp(m - m_new)
        acc[rows, :] = a_old * acc[rows, :] + a_new * o
        l_s[rows, :] = a_old * l_s[rows, :] + a_new * l
        m_s[rows, :] = m_new

    dil, radius, length, kw, nblk = geom[0]

    def fin0(i, h, o, m, l):
        rows = pl.ds(pl.multiple_of(i * tq, tq), tq)
        o_ref[rows, :] = o
        tmp_a[rows, :] = m
        tmp_b[rows, :] = l

    _band_attention(nblk, grp, nblk, kw, radius, 1, lambda h, rows: q_ref[rows, :], k_ref, v_ref, b1, None,
                    s_scr, p_scr, st_scr, fin0)
    for r in range(4):
        acc[r * len4:(r + 1) * len4, :] = o_ref[pl.ds(r, len4, stride=4), :]
        m_s[r * len4:(r + 1) * len4, :] = tmp_a[pl.ds(r, len4, stride=4), :]
        l_s[r * len4:(r + 1) * len4, :] = tmp_b[pl.ds(r, len4, stride=4), :]

    dil, radius, length, kw, nblk = geom[1]

    def fin4(i, h, o, m, l):
        merge(pl.ds(pl.multiple_of(i * tq, tq), tq), o, m, l)

    _band_attention(seq // tq, grp, nblk, kw, radius, 1, lambda h, rows: q4[rows, :], k4, v4, b2, None,
                    s_scr, p_scr, st_scr, fin4)

    dil, radius, length, kw, nblk = geom[2]

    def fin16(i, h, o, m, l):
        r16 = i // nblk
        l0 = (i % nblk) * tq
        merge(pl.ds((r16 % 4) * len4 + 4 * l0 + r16 // 4, tq, stride=4), o, m, l)

    _band_attention(seq // tq, grp, nblk, kw, radius, 1, lambda h, rows: q16[rows, :], k16, v16, b3, None,
                    s_scr, p_scr, st_scr, fin16)

    for r in range(4):
        o_ref[pl.ds(r, len4, stride=4), :] = acc[r * len4:(r + 1) * len4, :] / l_s[r * len4:(r + 1) * len4, :]


def _attn_a_call(buf, proj, slopes_a, seq, n_seq, row_block0):
    n = proj.shape[0]
    assert seq % (16 * ATTN_TQ) == 0
    bias_shapes = []
    kw_max = 0
    for window, dil in DILATED_PATTERNS:
        kw = min(ATTN_TQ + window // dil, seq // dil)
        kw_max = max(kw_max, kw)
        bias_shapes.append(pltpu.VMEM((3, ATTN_TQ, kw), F32))

    def spec(col0):
        return pl.BlockSpec((seq, HEAD_DIM), lambda h, b: (row_block0 + b, col0 + h))

    if buf is None:
        buf = jnp.zeros((8, 128), F32)
        aliases = {}
    else:
        aliases = {1: 0}
    rows_bf = pltpu.VMEM((seq, HEAD_DIM), BF16)
    rows_f32 = pltpu.VMEM((seq, HEAD_DIM), F32)
    return pl.pallas_call(
        functools.partial(_attn_a_kernel, seq=seq),
        grid=(N_HEADS_A, n_seq),
        in_specs=[
            pl.BlockSpec(memory_space=pltpu.SMEM),
            pl.BlockSpec(memory_space=pl.ANY),
            spec(0), spec(N_HEADS_A), spec(2 * N_HEADS_A),
        ],
        out_specs=pl.BlockSpec((seq, HEAD_DIM), lambda h, b: (row_block0 + b, h)),
        out_shape=jax.ShapeDtypeStruct((n, D_A + D_B), F32),
        input_output_aliases=aliases,
        scratch_shapes=[rows_bf] * 6 + [rows_f32] * 5 + bias_shapes + _pipeline_scratch(ATTN_A_GROUP, kw_max),
        compiler_params=_params("arbitrary", "arbitrary"),
        name=f"attn_a_s{seq}",
    )(slopes_a, buf, proj, proj, proj)


def _attn_b_kernel(slopes_ref, sink_ref, buf_ref, q_ref, k_ref, v_ref, o_ref, bias, s_scr, p_scr, st_scr,
                   *, seq, layer):
    del buf_ref
    tq = ATTN_TQ
    kw = tq + 2 * SWA_RADIUS
    nblk = seq // tq
    g = pl.program_id(0)

    @pl.when(pl.program_id(1) == 0)
    def _():
        for hh in range(REP_B):
            _fill_bias(bias, 3 * hh, slopes_ref[g * REP_B + hh], 1, SWA_RADIUS, kw)

    def fin(i, h, o, m, l):
        o_ref[pl.ds(pl.multiple_of(i * tq, tq), tq), h * HEAD_DIM:(h + 1) * HEAD_DIM] = o / l

    _band_attention(nblk, ATTN_B_GROUP, nblk, kw, SWA_RADIUS, REP_B,
                    lambda h, rows: q_ref[rows, h * HEAD_DIM:(h + 1) * HEAD_DIM], k_ref, v_ref, bias,
                    lambda h: sink_ref[layer, g * REP_B + h], s_scr, p_scr, st_scr, fin)


def _attn_b_call(buf, proj, slopes_b, sink_b, layer, seq, n_seq, row_block0):
    n = proj.shape[0]
    kw = ATTN_TQ + 2 * SWA_RADIUS
    assert seq % ATTN_TQ == 0 and seq >= kw
    wq = REP_B * HEAD_DIM
    q_col0 = 3 * D_A // wq
    k_col0 = (3 * D_A + D_B) // HEAD_DIM
    v_col0 = (3 * D_A + D_B + D_KV_B) // HEAD_DIM
    return pl.pallas_call(
        functools.partial(_attn_b_kernel, seq=seq, layer=layer),
        grid=(N_KV_B, n_seq),
        in_specs=[
            pl.BlockSpec(memory_space=pltpu.SMEM),
            pl.BlockSpec(memory_space=pltpu.SMEM),
            pl.BlockSpec(memory_space=pl.ANY),
            pl.BlockSpec((seq, wq), lambda g, b: (row_block0 + b, q_col0 + g)),
            pl.BlockSpec((seq, HEAD_DIM), lambda g, b: (row_block0 + b, k_col0 + g)),
            pl.BlockSpec((seq, HEAD_DIM), lambda g, b: (row_block0 + b, v_col0 + g)),
        ],
        out_specs=pl.BlockSpec((seq, wq), lambda g, b: (row_block0 + b, D_A // wq + g)),
        out_shape=jax.ShapeDtypeStruct((n, D_A + D_B), F32),
        input_output_aliases={2: 0},
        scratch_shapes=[pltpu.VMEM((3 * REP_B, ATTN_TQ, kw), F32)] + _pipeline_scratch(ATTN_B_GROUP * REP_B, kw),
        compiler_params=_params("arbitrary", "arbitrary"),
        name=f"attn_b_s{seq}_l{layer}",
    )(slopes_b, sink_b, buf, proj, proj, proj)


def _route(logits):
    lane = lax.broadcasted_iota(jnp.int32, logits.shape, 1).astype(F32)
    far = float(ROUTER_LANES)
    gl = jnp.where(lane < N_GROUPS, logits, NEG_INF)
    gmax = jnp.max(gl, axis=-1, keepdims=True)
    gsum = jnp.sum(jnp.exp(gl - gmax), axis=-1, keepdims=True)
    g_gate = 1.0 / gsum
    gidx = jnp.min(jnp.where(gl == gmax, lane, far), axis=-1, keepdims=True)
    base = N_GROUPS + N_EXPERTS_PER_GROUP * gidx
    el = jnp.where((lane >= base) & (lane < base + N_EXPERTS_PER_GROUP), logits, NEG_INF)
    m1 = jnp.max(el, axis=-1, keepdims=True)
    i1 = jnp.min(jnp.where(el == m1, lane, far), axis=-1, keepdims=True)
    el2 = jnp.where(lane == i1, NEG_INF, el)
    m2 = jnp.max(el2, axis=-1, keepdims=True)
    i2 = jnp.min(jnp.where(el2 == m2, lane, far), axis=-1, keepdims=True)
    e2 = jnp.exp(m2 - m1)
    w1 = 1.0 / (1.0 + e2)
    w2 = e2 / (1.0 + e2)
    first = lane - ROUTE_W0 + base
    weights = jnp.where(first == i1, g_gate * w1, jnp.where(first == i2, g_gate * w2, 0.0))
    return jnp.where(lane == 0.0, gidx, weights)


def _outproj_kernel(a_ref, x_ref, wo_ref, gn_ref, g1_ref, sc2_ref, sh2_ref, lng_ref, lnb_ref,
                    wr_ref, br_ref, x1_ref, cw_ref):
    na = _rms_norm(a_ref[:, :D_A], gn_ref[:, :D_A]).astype(BF16)
    nb = _rms_norm(a_ref[:, D_A:], gn_ref[:, D_A:]).astype(BF16)
    mix = jnp.dot(na, wo_ref[:D_A, :], preferred_element_type=F32)
    mix = mix + jnp.dot(nb, wo_ref[D_A:, :], preferred_element_type=F32)
    y = DEEPNORM_ALPHA * x_ref[...] + (1.0 + g1_ref[...]) * mix
    x1 = _layer_norm(y, lng_ref[...], lnb_ref[...])
    x1_ref[...] = x1
    h2 = (x1 * (1.0 + sc2_ref[...]) + sh2_ref[...]).astype(BF16)
    logits = jnp.dot(h2, wr_ref[...], preferred_element_type=F32) + br_ref[...]
    cw_ref[...] = _route(logits)


def _outproj_call(tok, layer, attn, x, mod5, w_out_bf, gn, ln_g, ln_b, w_router_bf, b_router):
    n, d = x.shape
    tm = 512
    const = lambda shape: pl.BlockSpec(shape, lambda i: (layer,) + (0,) * (len(shape) - 1))
    return pl.pallas_call(
        _outproj_kernel,
        grid=(n // tm,),
        in_specs=[
            pl.BlockSpec((tm, D_A + D_B), lambda i: (i, 0)),
            pl.BlockSpec((tm, d), lambda i: (i, 0)),
            const((None, D_A + D_B, d)),
            const((None, 1, D_A + D_B)),
            _mod_spec(tok, tm, layer, 2, d, 1),
            _mod_spec(tok, tm, layer, 4, d, 1),
            _mod_spec(tok, tm, layer, 3, d, 1),
            const((None, 1, d)),
            const((None, 1, d)),
            const((None, d, ROUTER_LANES)),
            const((None, 1, ROUTER_LANES)),
        ],
        out_specs=[
            pl.BlockSpec((tm, d), lambda i: (i, 0)),
            pl.BlockSpec((tm, ROUTER_LANES), lambda i: (i, 0)),
        ],
        out_shape=[
            jax.ShapeDtypeStruct((n, d), F32),
            jax.ShapeDtypeStruct((n, ROUTER_LANES), F32),
        ],
        compiler_params=_params("arbitrary"),
        name=f"outproj_l{layer}",
    )(attn, x, w_out_bf, gn, mod5, mod5, mod5, ln_g, ln_b, w_router_bf, b_router)


def _moe_plan(route, n):
    tm = MOE_TM
    n_tiles = n // tm + N_GROUPS
    groups = jnp.arange(N_GROUPS, dtype=jnp.int32)
    gi = route[:, 0].astype(jnp.int32)
    onehot = (gi[:, None] == groups[None, :]).astype(jnp.int32)
    csum = jnp.cumsum(onehot, axis=0)
    counts = csum[-1]
    padded = (counts + tm - 1) // tm * tm
    ends = jnp.cumsum(padded)
    base = ends - padded
    rank = jnp.sum(csum * onehot, axis=1) - 1
    pos = jnp.sum(base[None, :] * onehot, axis=1) + rank
    k = jnp.arange(N_GROUPS * tm, dtype=jnp.int32)
    kg, kr = k // tm, k % tm
    pad_slot = base[kg] + counts[kg] + kr
    pad_pos = jnp.where(counts[kg] + kr < padded[kg], pad_slot, n_tiles * tm + k)
    tile_start = jnp.arange(n_tiles, dtype=jnp.int32) * tm
    tile_group = jnp.minimum(jnp.sum((tile_start[:, None] >= ends[None, :]).astype(jnp.int32), axis=1), N_GROUPS - 1)
    n_used = (ends[-1] // tm).reshape(1)
    return pos, pad_pos, tile_group, n_used, n_tiles


def _dispatch_kernel(pos_ref, pad_ref, x_ref, rt_ref, sc2_ref, sh2_ref, hs_ref, buf, zero, sems, zsem):
    i = pl.program_id(0)
    n = pl.num_programs(0)
    tm, d = x_ref.shape
    slot = i % 2

    def wait_slot(s):
        pltpu.make_async_copy(buf.at[s], hs_ref.at[pl.ds(0, tm), :], sems.at[s]).wait()

    @pl.when(i >= 2)
    def _():
        wait_slot(slot)

    buf[slot, :, :d] = x_ref[...] * (1.0 + sc2_ref[...]) + sh2_ref[...]
    buf[slot, :, d:] = rt_ref[...]

    def issue(t, carry):
        pltpu.make_async_copy(buf.at[slot, pl.ds(t, 1), :], hs_ref.at[pl.ds(pos_ref[0, t], 1), :],
                              sems.at[slot]).start()
        return carry

    lax.fori_loop(0, tm, issue, 0, unroll=DMA_ISSUE_UNROLL)

    @pl.when(i == n - 1)
    def _():
        n_pad = pad_ref.shape[0]
        zero[...] = jnp.zeros_like(zero)

        def issue_pad(t, carry):
            pltpu.make_async_copy(zero.at[pl.ds(0, 1), :], hs_ref.at[pl.ds(pad_ref[t], 1), :], zsem.at[0]).start()
            return carry

        lax.fori_loop(0, n_pad, issue_pad, 0)

        def wait_pad(t, carry):
            pltpu.make_async_copy(zero.at[pl.ds(0, 1), :], hs_ref.at[pl.ds(0, 1), :], zsem.at[0]).wait()
            return carry

        lax.fori_loop(0, n_pad, wait_pad, 0)
        wait_slot(slot)

        @pl.when(n >= 2)
        def _():
            wait_slot(1 - slot)


def _dispatch_call(tok, layer, x1, route, mod5, pos, pad_pos, n_slots):
    n, d = x1.shape
    tm = 512
    w = d + ROUTER_LANES
    return pl.pallas_call(
        _dispatch_kernel,
        grid=(n // tm,),
        in_specs=[
            pl.BlockSpec((None, 1, tm), lambda i: (i, 0, 0), memory_space=pltpu.SMEM),
            pl.BlockSpec(memory_space=pltpu.SMEM),
            pl.BlockSpec((tm, d), lambda i: (i, 0)),
            pl.BlockSpec((tm, ROUTER_LANES), lambda i: (i, 0)),
            _mod_spec(tok, tm, layer, 4, d, 1),
            _mod_spec(tok, tm, layer, 3, d, 1),
        ],
        out_specs=pl.BlockSpec(memory_space=pl.ANY),
        out_shape=jax.ShapeDtypeStruct((n_slots, w), F32),
        scratch_shapes=[pltpu.VMEM((2, tm, w), F32), pltpu.VMEM((8, w), F32),
                        pltpu.SemaphoreType.DMA((2,)), pltpu.SemaphoreType.DMA((1,))],
        compiler_params=_params("arbitrary"),
        name=f"moe_dispatch_l{layer}",
    )(pos.reshape(n // tm, 1, tm), pad_pos, x1, route, mod5, mod5)


def _experts_kernel(tg_ref, nu_ref, hs_ref, wg_ref, wu_ref, wd_ref, ys_ref):
    del tg_ref
    d = ys_ref.shape[-1]
    f = wd_ref.shape[0] // N_EXPERTS_PER_GROUP

    @pl.when(pl.program_id(0) < nu_ref[0])
    def _():
        h = hs_ref[:, :d].astype(BF16)
        a = jnp.dot(h, wg_ref[...], preferred_element_type=F32)
        u = jnp.dot(h, wu_ref[...], preferred_element_type=F32)
        hid = (a * jax.nn.sigmoid(a)) * u
        parts = [hid[:, e * f:(e + 1) * f] * hs_ref[:, d + ROUTE_W0 + e:d + ROUTE_W0 + e + 1]
                 for e in range(N_EXPERTS_PER_GROUP)]
        ys_ref[...] = jnp.dot(jnp.concatenate(parts, axis=-1).astype(BF16), wd_ref[...],
                              preferred_element_type=F32)


def _experts_call(layer, hs, tile_group, n_used, n_tiles, wg4, wu4, wd4):
    tm = MOE_TM
    w = hs.shape[-1]
    d = wd4.shape[-1]
    ef = wg4.shape[-1]
    row = lambda i, tg, nu: (jnp.minimum(i, nu[0] - 1), 0)
    wspec = lambda shape: pl.BlockSpec(shape, lambda i, tg, nu: (layer, tg[jnp.minimum(i, nu[0] - 1)], 0, 0))
    return pl.pallas_call(
        _experts_kernel,
        grid_spec=pltpu.PrefetchScalarGridSpec(
            num_scalar_prefetch=2,
            grid=(n_tiles,),
            in_specs=[
                pl.BlockSpec((tm, w), row),
                wspec((None, None, d, ef)),
                wspec((None, None, d, ef)),
                wspec((None, None, ef, d)),
            ],
            out_specs=pl.BlockSpec((tm, d), row),
        ),
        out_shape=jax.ShapeDtypeStruct((n_tiles * tm, d), F32),
        compiler_params=_params("arbitrary"),
        name=f"moe_experts_l{layer}",
    )(tile_group, n_used, hs, wg4, wu4, wd4)


def _combine_kernel(pos_ref, posn_ref, x_ref, g2_ref, lng_ref, lnb_ref, ys_ref, *rest, n_prompt_tiles):
    if n_prompt_tiles is None:
        o_ref, buf, sems = rest
    else:
        op_ref, os_ref, buf, sems = rest
    i = pl.program_id(0)
    n = pl.num_programs(0)
    tm = x_ref.shape[0]
    slot = i % 2

    def gather(p_ref, s):
        def issue(t, carry):
            pltpu.make_async_copy(ys_ref.at[pl.ds(p_ref[0, t], 1), :], buf.at[s, pl.ds(t, 1), :], sems.at[s]).start()
            return carry

        lax.fori_loop(0, tm, issue, 0, unroll=DMA_ISSUE_UNROLL)

    @pl.when(i == 0)
    def _():
        gather(pos_ref, slot)

    @pl.when(i + 1 < n)
    def _():
        gather(posn_ref, 1 - slot)

    pltpu.make_async_copy(ys_ref.at[pl.ds(0, tm), :], buf.at[slot], sems.at[slot]).wait()

    def finish():
        y = DEEPNORM_ALPHA * x_ref[...] + (1.0 + g2_ref[...]) * buf[slot]
        return _layer_norm(y, lng_ref[...], lnb_ref[...])

    if n_prompt_tiles is None:
        o_ref[...] = finish()
    else:
        @pl.when(i < n_prompt_tiles)
        def _():
            op_ref[...] = finish()

        @pl.when(i >= n_prompt_tiles)
        def _():
            os_ref[...] = finish()


def _combine_call(tok, layer, final, x1, ys, pos, mod5, ln_g, ln_b):
    n, d = x1.shape
    tm = 512
    nt = n // tm
    npt = tok.n_prompt // tm
    const = lambda shape: pl.BlockSpec(shape, lambda i: (layer,) + (0,) * (len(shape) - 1))
    if final:
        out_specs = [
            pl.BlockSpec((tm, d), lambda i: (jnp.minimum(i, npt - 1), 0)),
            pl.BlockSpec((tm, d), lambda i: (jnp.maximum(i - npt, 0), 0)),
        ]
        out_shape = [jax.ShapeDtypeStruct((tok.n_prompt, d), F32), jax.ShapeDtypeStruct((tok.n_sample, d), F32)]
    else:
        out_specs = pl.BlockSpec((tm, d), lambda i: (i, 0))
        out_shape = jax.ShapeDtypeStruct((n, d), F32)
    pos3 = pos.reshape(nt, 1, tm)
    return pl.pallas_call(
        functools.partial(_combine_kernel, n_prompt_tiles=npt if final else None),
        grid=(nt,),
        in_specs=[
            pl.BlockSpec((None, 1, tm), lambda i: (i, 0, 0), memory_space=pltpu.SMEM),
            pl.BlockSpec((None, 1, tm), lambda i: (jnp.minimum(i + 1, nt - 1), 0, 0), memory_space=pltpu.SMEM),
            pl.BlockSpec((tm, d), lambda i: (i, 0)),
            _mod_spec(tok, tm, layer, 5, d, 1),
            const((None, 1, d)),
            const((None, 1, d)),
            pl.BlockSpec(memory_space=pl.ANY),
        ],
        out_specs=out_specs,
        out_shape=out_shape,
        scratch_shapes=[pltpu.VMEM((2, tm, d), F32), pltpu.SemaphoreType.DMA((2,))],
        compiler_params=_params("arbitrary"),
        name=f"moe_combine_l{layer}",
    )(pos3, pos3, x1, mod5, ln_g, ln_b, ys)


def kernel(x_prompt, x_sample, c_prompt, c_sample, ln_in_g, ln_in_b, w_ada, b_ada, w_in, sink_b, gn_a, gn_b,
           w_out, ln1_g, ln1_b, w_rg, b_rg, w_re, b_re, w_gate, w_up, w_down, ln2_g, ln2_b):
    bp, sp, d = x_prompt.shape
    bs, ss, _ = x_sample.shape
    depth = w_in.shape[0]
    tok = _Tokens(bp, sp, bs, ss)
    nb = bp + bs
    nbp = -(-nb // 8) * 8
    assert tok.n_prompt % ss == 0

    slopes_a, slopes_b = (jnp.asarray(s) for s in _alibi_slopes())
    w_in_bf = w_in.astype(BF16)
    w_out_bf = w_out.astype(BF16)
    g, e, f = N_GROUPS, N_EXPERTS_PER_GROUP, w_gate.shape[-1]
    wg4 = jnp.transpose(w_gate.astype(BF16), (0, 1, 3, 2, 4)).reshape(depth, g, d, e * f)
    wu4 = jnp.transpose(w_up.astype(BF16), (0, 1, 3, 2, 4)).reshape(depth, g, d, e * f)
    wd4 = w_down.astype(BF16).reshape(depth, g, e * f, d)
    gn = jnp.concatenate([gn_a, gn_b], axis=-1).reshape(depth, 1, D_A + D_B)
    w_router_bf = jnp.concatenate(
        [w_rg, jnp.transpose(w_re, (0, 2, 1, 3)).reshape(depth, d, g * e),
         jnp.zeros((depth, d, ROUTER_LANES - g - g * e), F32)], axis=-1).astype(BF16)
    b_router = jnp.concatenate(
        [b_rg, b_re.reshape(depth, g * e), jnp.zeros((depth, ROUTER_LANES - g - g * e), F32)],
        axis=-1).reshape(depth, 1, ROUTER_LANES)
    ln1g, ln1b = ln1_g.reshape(depth, 1, d), ln1_b.reshape(depth, 1, d)
    ln2g, ln2b = ln2_g.reshape(depth, 1, d), ln2_b.reshape(depth, 1, d)

    c_pad = jnp.concatenate([c_prompt, c_sample, jnp.zeros((nbp - nb, d), F32)], axis=0)
    mod5 = _ada_call(c_pad, w_ada, b_ada).reshape(depth, nbp, 6, 1, d)

    x = _ln_in_call(tok, x_prompt.reshape(bp * sp, d), x_sample.reshape(bs * ss, d), ln_in_g, ln_in_b)

    sample_block0 = tok.n_prompt // ss
    for layer in range(depth):
        proj = _inproj_call(tok, layer, x, mod5, w_in_bf)
        attn = _attn_a_call(None, proj, slopes_a, sp, bp, 0)
        attn = _attn_a_call(attn, proj, slopes_a, ss, bs, sample_block0)
        attn = _attn_b_call(attn, proj, slopes_b, sink_b, layer, sp, bp, 0)
        attn = _attn_b_call(attn, proj, slopes_b, sink_b, layer, ss, bs, sample_block0)
        x1, route = _outproj_call(tok, layer, attn, x, mod5, w_out_bf, gn, ln1g, ln1b, w_router_bf, b_router)
        pos, pad_pos, tile_group, n_used, n_tiles = _moe_plan(route, tok.n)
        hs = _dispatch_call(tok, layer, x1, route, mod5, pos, pad_pos, (n_tiles + N_GROUPS) * MOE_TM)
        ys = _experts_call(layer, hs, tile_group, n_used, n_tiles, wg4, wu4, wd4)
        x = _combine_call(tok, layer, layer == depth - 1, x1, ys, pos, mod5, ln2g, ln2b)

    y_prompt, y_sample = x
    return y_prompt.reshape(bp, sp, d), y_sample.reshape(bs, ss, d)
```

```python
import functools
import math

import numpy as np
import jax
import jax.numpy as jnp
from jax import lax
from jax.experimental import pallas as pl
from jax.experimental.pallas import tpu as pltpu

HEAD_DIM = 128
N_HEADS_A = 8
N_HEADS_B = 8
N_KV_B = 2
REP_B = N_HEADS_B // N_KV_B
D_A = N_HEADS_A * HEAD_DIM
D_B = N_HEADS_B * HEAD_DIM
D_KV_B = N_KV_B * HEAD_DIM
D_IN = 3 * D_A + D_B + 2 * D_KV_B
DILATED_PATTERNS = ((128, 1), (512, 4), (2048, 16))
SWA_RADIUS = 128
N_GROUPS = 4
N_EXPERTS_PER_GROUP = 4
N_EXPERTS = N_GROUPS * N_EXPERTS_PER_GROUP
MODEL_DEPTH = 4
DEEPNORM_ALPHA = (2 * MODEL_DEPTH) ** 0.25
LN_EPS = 1e-5
NEG_INF = -1e30
ATTN_SCALE = HEAD_DIM ** -0.5

ROUTER_LANES = 128
ROUTE_W0 = 4
MOE_TM = 512
DMA_ISSUE_UNROLL = 8
ATTN_TQ = 128
ATTN_A_GROUP = 4
ATTN_B_GROUP = 1
VMEM_LIMIT = 56 * 1024 * 1024

F32 = jnp.float32
BF16 = jnp.bfloat16


def _alibi_slopes():
    n = N_HEADS_A + N_HEADS_B
    s = (2.0 ** (-8.0 * np.arange(1, n + 1, dtype=np.float32) / n)).astype(np.float32)
    return s[0::2], s[1::2]


def _params(*sem):
    return pltpu.CompilerParams(dimension_semantics=sem, vmem_limit_bytes=VMEM_LIMIT)


def _layer_norm(x, g, b):
    mu = jnp.mean(x, axis=-1, keepdims=True)
    xc = x - mu
    var = jnp.mean(xc * xc, axis=-1, keepdims=True)
    return xc * lax.rsqrt(var + LN_EPS) * g + b


def _rms_norm(x, g):
    return x * lax.rsqrt(jnp.mean(x * x, axis=-1, keepdims=True) + LN_EPS) * g


def _ada_kernel(c_ref, w_ref, b_ref, o_ref):
    c = c_ref[...]
    a = (c * jax.nn.sigmoid(c)).astype(BF16)
    o_ref[...] = jnp.dot(a, w_ref[...].astype(BF16), preferred_element_type=F32) + b_ref[...]


def _ada_call(c_pad, w_ada, b_ada):
    depth, d, e = w_ada.shape
    nbp = c_pad.shape[0]
    tn = 1024
    return pl.pallas_call(
        _ada_kernel,
        grid=(depth, e // tn),
        in_specs=[
            pl.BlockSpec((nbp, d), lambda l, j: (0, 0)),
            pl.BlockSpec((None, d, tn), lambda l, j: (l, 0, j)),
            pl.BlockSpec((None, 1, tn), lambda l, j: (l, 0, j)),
        ],
        out_specs=pl.BlockSpec((None, nbp, tn), lambda l, j: (l, 0, j)),
        out_shape=jax.ShapeDtypeStruct((depth, nbp, e), F32),
        compiler_params=_params("arbitrary", "arbitrary"),
        name="ada_mod",
    )(c_pad, w_ada, b_ada.reshape(depth, 1, e))


class _Tokens:
    def __init__(self, bp, sp, bs, ss):
        self.bp, self.sp, self.bs, self.ss = bp, sp, bs, ss
        self.n_prompt = bp * sp
        self.n_sample = bs * ss
        self.n = bp * sp + bs * ss

    def batch_of_tile(self, i, tm):
        assert self.sp % tm == 0 and self.ss % tm == 0
        npt = self.n_prompt // tm
        return jnp.where(i < npt, i // (self.sp // tm), self.bp + (i - npt) // (self.ss // tm))


def _mod_spec(tok, tm, layer, k, d, n_grid):
    if n_grid == 1:
        return pl.BlockSpec((None, None, None, 1, d), lambda i: (layer, tok.batch_of_tile(i, tm), k, 0, 0))
    return pl.BlockSpec((None, None, None, 1, d), lambda i, j: (layer, tok.batch_of_tile(i, tm), k, 0, 0))


def _ln_in_kernel(xp_ref, xs_ref, g_ref, b_ref, o_ref, *, n_prompt_tiles):
    i = pl.program_id(0)

    @pl.when(i < n_prompt_tiles)
    def _():
        o_ref[...] = _layer_norm(xp_ref[...], g_ref[...], b_ref[...])

    @pl.when(i >= n_prompt_tiles)
    def _():
        o_ref[...] = _layer_norm(xs_ref[...], g_ref[...], b_ref[...])


def _ln_in_call(tok, xp, xs, g, b):
    d = xp.shape[-1]
    tm = 512
    npt = tok.n_prompt // tm
    return pl.pallas_call(
        functools.partial(_ln_in_kernel, n_prompt_tiles=npt),
        grid=(tok.n // tm,),
        in_specs=[
            pl.BlockSpec((tm, d), lambda i: (jnp.minimum(i, npt - 1), 0)),
            pl.BlockSpec((tm, d), lambda i: (jnp.maximum(i - npt, 0), 0)),
            pl.BlockSpec((1, d), lambda i: (0, 0)),
            pl.BlockSpec((1, d), lambda i: (0, 0)),
        ],
        out_specs=pl.BlockSpec((tm, d), lambda i: (i, 0)),
        out_shape=jax.ShapeDtypeStruct((tok.n, d), F32),
        compiler_params=_params("arbitrary"),
        name="ln_in",
    )(xp, xs, g.reshape(1, d), b.reshape(1, d))


def _inproj_kernel(x_ref, sh_ref, sc_ref, w_ref, o_ref, h_scr):
    @pl.when(pl.program_id(1) == 0)
    def _():
        h_scr[...] = (x_ref[...] * (1.0 + sc_ref[...]) + sh_ref[...]).astype(BF16)

    o_ref[...] = jnp.dot(h_scr[...], w_ref[...], preferred_element_type=F32).astype(o_ref.dtype)


def _inproj_call(tok, layer, x, mod5, w_in_bf):
    n, d = x.shape
    tm, tn = 1024, 1536
    return pl.pallas_call(
        _inproj_kernel,
        grid=(n // tm, D_IN // tn),
        in_specs=[
            pl.BlockSpec((tm, d), lambda i, j: (i, 0)),
            _mod_spec(tok, tm, layer, 0, d, 2),
            _mod_spec(tok, tm, layer, 1, d, 2),
            pl.BlockSpec((None, d, tn), lambda i, j: (layer, 0, j)),
        ],
        out_specs=pl.BlockSpec((tm, tn), lambda i, j: (i, j)),
        out_shape=jax.ShapeDtypeStruct((n, D_IN), BF16),
        scratch_shapes=[pltpu.VMEM((tm, d), BF16)],
        compiler_params=_params("arbitrary", "arbitrary"),
        name=f"inproj_l{layer}",
    )(x, mod5, mod5, w_in_bf)


def _band_offsets(radius, kw):
    return (0, -radius, ATTN_TQ - kw)


def _fill_bias(bias_ref, base, slope, dil, radius, kw):
    row = lax.broadcasted_iota(jnp.int32, (ATTN_TQ, kw), 0)
    col = lax.broadcasted_iota(jnp.int32, (ATTN_TQ, kw), 1)
    for v, off in enumerate(_band_offsets(radius, kw)):
        a = jnp.abs(col - row + off)
        bias = -slope * (a * dil).astype(F32)
        bias_ref[base + v] = jnp.where(a <= radius, bias, NEG_INF)


def _band_attention(n_blocks, grp, nblk, kw, radius, heads, q_of, k_ref, v_ref, bias_ref, sink_of,
                    s_scr, p_scr, st_scr, finalize):
    tq = ATTN_TQ
    assert n_blocks % grp == 0
    n_it = n_blocks // grp
    offs = _band_offsets(radius, kw)
    align = math.gcd(tq, *[abs(o) for o in offs if o])

    def geom(it, j):
        i = it * grp + j
        variant = jnp.where(i % nblk == 0, 0, jnp.where(i % nblk == nblk - 1, 2, 1))
        off = jnp.where(variant == 0, offs[0], jnp.where(variant == 1, offs[1], offs[2]))
        q0 = pl.multiple_of(i * tq, tq)
        k0 = pl.multiple_of(i * tq + off, align)
        return i, variant, q0, k0

    def qk(it):
        it = jnp.asarray(it, jnp.int32)
        slot = it % 2
        for j in range(grp):
            _, variant, q0, k0 = geom(it, j)
            kwin = k_ref[pl.ds(k0, kw), :]
            for h in range(heads):
                s = lax.dot_general(q_of(h, pl.ds(q0, tq)), kwin, (((1,), (1,)), ((), ())),
                                    preferred_element_type=F32)
                s_scr[slot, j * heads + h, :, :kw] = s * ATTN_SCALE + bias_ref[3 * h + variant]

    def sm(it):
        it = jnp.asarray(it, jnp.int32)
        slot = it % 2
        for e in range(grp * heads):
            s = s_scr[slot, e, :, :kw]
            m = jnp.max(s, axis=-1, keepdims=True)
            sink = None if sink_of is None else sink_of(e % heads)
            if sink is not None:
                m = jnp.maximum(m, sink)
            p = jnp.exp(s - m)
            l = jnp.sum(p, axis=-1, keepdims=True)
            if sink is not None:
                l = l + jnp.exp(sink - m)
            p_scr[slot, e, :, :kw] = p.astype(BF16)
            st_scr[slot, e, 0] = jnp.broadcast_to(m, (tq, HEAD_DIM))
            st_scr[slot, e, 1] = jnp.broadcast_to(l, (tq, HEAD_DIM))

    def pv(it):
        it = jnp.asarray(it, jnp.int32)
        slot = it % 2
        for j in range(grp):
            i, _, _, k0 = geom(it, j)
            vwin = v_ref[pl.ds(k0, kw), :]
            for h in range(heads):
                e = j * heads + h
                o = jnp.dot(p_scr[slot, e, :, :kw], vwin, preferred_element_type=F32)
                finalize(i, h, o, st_scr[slot, e, 0], st_scr[slot, e, 1])

    qk(0)
    sm(0)
    if n_it > 1:
        qk(1)

        def body(it, carry):
            pv(it - 1)
            sm(it)
            qk(jnp.minimum(it + 1, n_it - 1))
            return carry

        lax.fori_loop(1, n_it, body, 0)
    pv(n_it - 1)


def _pipeline_scratch(entries, kw):
    return [pltpu.VMEM((2, entries, ATTN_TQ, kw), F32), pltpu.VMEM((2, entries, ATTN_TQ, kw), BF16),
            pltpu.VMEM((2, entries, 2, ATTN_TQ, HEAD_DIM), F32)]


def _attn_a_kernel(slopes_ref, buf_ref, q_ref, k_ref, v_ref, o_ref,
                   q4, k4, v4, q16, k16, v16, tmp_a, tmp_b, acc, m_s, l_s, b1, b2, b3,
                   s_scr, p_scr, st_scr, *, seq):
    del buf_ref
    tq = ATTN_TQ
    grp = ATTN_A_GROUP
    bias_refs = (b1, b2, b3)
    geom = []
    for window, dil in DILATED_PATTERNS:
        radius = window // (2 * dil)
        length = seq // dil
        geom.append((dil, radius, length, min(tq + 2 * radius, length), length // tq))
    assert [g[0] for g in geom] == [1, 4, 16]
    len4, len16 = seq // 4, seq // 16

    @pl.when(pl.program_id(1) == 0)
    def _():
        slope = slopes_ref[pl.program_id(0)]
        for (dil, radius, length, kw, nblk), bref in zip(geom, bias_refs):
            _fill_bias(bref, 0, slope, dil, radius, kw)

    for src, dst4, dst16 in ((q_ref, q4, q16), (k_ref, k4, k16), (v_ref, v4, v16)):
        tmp_a[...] = src[...].astype(F32)
        for r in range(4):
            cls = tmp_a[pl.ds(r, len4, stride=4), :]
            tmp_b[r * len4:(r + 1) * len4, :] = cls
            dst4[r * len4:(r + 1) * len4, :] = cls.astype(BF16)
        for r in range(16):
            dst16[r * len16:(r + 1) * len16, :] = tmp_b[pl.ds((r % 4) * len4 + r // 4, len16, stride=4), :].astype(BF16)

    def merge(rows, o, m, l):
        m_old = m_s[rows, :]
        m_new = jnp.maximum(m_old, m)
        a_old = jnp.exp(m_old - m_new)
        a_new = jnp.exp(m - m_new)
        acc[rows, :] = a_old * acc[rows, :] + a_new * o
        l_s[rows, :] = a_old * l_s[rows, :] + a_new * l
        m_s[rows, :] = m_new

    dil, radius, length, kw, nblk = geom[0]

    def fin0(i, h, o, m, l):
        rows = pl.ds(pl.multiple_of(i * tq, tq), tq)
        o_ref[rows, :] = o
        tmp_a[rows, :] = m
        tmp_b[rows, :] = l

    _band_attention(nblk, grp, nblk, kw, radius, 1, lambda h, rows: q_ref[rows, :], k_ref, v_ref, b1, None,
                    s_scr, p_scr, st_scr, fin0)
    for r in range(4):
        acc[r * len4:(r + 1) * len4, :] = o_ref[pl.ds(r, len4, stride=4), :]
        m_s[r * len4:(r + 1) * len4, :] = tmp_a[pl.ds(r, len4, stride=4), :]
        l_s[r * len4:(r + 1) * len4, :] = tmp_b[pl.ds(r, len4, stride=4), :]

    dil, radius, length, kw, nblk = geom[1]

    def fin4(i, h, o, m, l):
        merge(pl.ds(pl.multiple_of(i * tq, tq), tq), o, m, l)

    _band_attention(seq // tq, grp, nblk, kw, radius, 1, lambda h, rows: q4[rows, :], k4, v4, b2, None,
                    s_scr, p_scr, st_scr, fin4)

    dil, radius, length, kw, nblk = geom[2]

    def fin16(i, h, o, m, l):
        r16 = i // nblk
        l0 = (i % nblk) * tq
        merge(pl.ds((r16 % 4) * len4 + 4 * l0 + r16 // 4, tq, stride=4), o, m, l)

    _band_attention(seq // tq, grp, nblk, kw, radius, 1, lambda h, rows: q16[rows, :], k16, v16, b3, None,
                    s_scr, p_scr, st_scr, fin16)

    for r in range(4):
        o_ref[pl.ds(r, len4, stride=4), :] = acc[r * len4:(r + 1) * len4, :] / l_s[r * len4:(r + 1) * len4, :]


def _attn_a_call(buf, proj, slopes_a, seq, n_seq, row_block0):
    n = proj.shape[0]
    assert seq % (16 * ATTN_TQ) == 0
    bias_shapes = []
    kw_max = 0
    for window, dil in DILATED_PATTERNS:
        kw = min(ATTN_TQ + window // dil, seq // dil)
        kw_max = max(kw_max, kw)
        bias_shapes.append(pltpu.VMEM((3, ATTN_TQ, kw), F32))

    def spec(col0):
        return pl.BlockSpec((seq, HEAD_DIM), lambda h, b: (row_block0 + b, col0 + h))

    if buf is None:
        buf = jnp.zeros((8, 128), F32)
        aliases = {}
    else:
        aliases = {1: 0}
    rows_bf = pltpu.VMEM((seq, HEAD_DIM), BF16)
    rows_f32 = pltpu.VMEM((seq, HEAD_DIM), F32)
    return pl.pallas_call(
        functools.partial(_attn_a_kernel, seq=seq),
        grid=(N_HEADS_A, n_seq),
        in_specs=[
            pl.BlockSpec(memory_space=pltpu.SMEM),
            pl.BlockSpec(memory_space=pl.ANY),
            spec(0), spec(N_HEADS_A), spec(2 * N_HEADS_A),
        ],
        out_specs=pl.BlockSpec((seq, HEAD_DIM), lambda h, b: (row_block0 + b, h)),
        out_shape=jax.ShapeDtypeStruct((n, D_A + D_B), F32),
        input_output_aliases=aliases,
        scratch_shapes=[rows_bf] * 6 + [rows_f32] * 5 + bias_shapes + _pipeline_scratch(ATTN_A_GROUP, kw_max),
        compiler_params=_params("arbitrary", "arbitrary"),
        name=f"attn_a_s{seq}",
    )(slopes_a, buf, proj, proj, proj)


def _attn_b_kernel(slopes_ref, sink_ref, buf_ref, q_ref, k_ref, v_ref, o_ref, bias, s_scr, p_scr, st_scr,
                   *, seq, layer):
    del buf_ref
    tq = ATTN_TQ
    kw = tq + 2 * SWA_RADIUS
    nblk = seq // tq
    g = pl.program_id(0)

    @pl.when(pl.program_id(1) == 0)
    def _():
        for hh in range(REP_B):
            _fill_bias(bias, 3 * hh, slopes_ref[g * REP_B + hh], 1, SWA_RADIUS, kw)

    def fin(i, h, o, m, l):
        o_ref[pl.ds(pl.multiple_of(i * tq, tq), tq), h * HEAD_DIM:(h + 1) * HEAD_DIM] = o / l

    _band_attention(nblk, ATTN_B_GROUP, nblk, kw, SWA_RADIUS, REP_B,
                    lambda h, rows: q_ref[rows, h * HEAD_DIM:(h + 1) * HEAD_DIM], k_ref, v_ref, bias,
                    lambda h: sink_ref[layer, g * REP_B + h], s_scr, p_scr, st_scr, fin)


def _attn_b_call(buf, proj, slopes_b, sink_b, layer, seq, n_seq, row_block0):
    n = proj.shape[0]
    kw = ATTN_TQ + 2 * SWA_RADIUS
    assert seq % ATTN_TQ == 0 and seq >= kw
    wq = REP_B * HEAD_DIM
    q_col0 = 3 * D_A // wq
    k_col0 = (3 * D_A + D_B) // HEAD_DIM
    v_col0 = (3 * D_A + D_B + D_KV_B) // HEAD_DIM
    return pl.pallas_call(
        functools.partial(_attn_b_kernel, seq=seq, layer=layer),
        grid=(N_KV_B, n_seq),
        in_specs=[
            pl.BlockSpec(memory_space=pltpu.SMEM),
            pl.BlockSpec(memory_space=pltpu.SMEM),
            pl.BlockSpec(memory_space=pl.ANY),
            pl.BlockSpec((seq, wq), lambda g, b: (row_block0 + b, q_col0 + g)),
            pl.BlockSpec((seq, HEAD_DIM), lambda g, b: (row_block0 + b, k_col0 + g)),
            pl.BlockSpec((seq, HEAD_DIM), lambda g, b: (row_block0 + b, v_col0 + g)),
        ],
        out_specs=pl.BlockSpec((seq, wq), lambda g, b: (row_block0 + b, D_A // wq + g)),
        out_shape=jax.ShapeDtypeStruct((n, D_A + D_B), F32),
        input_output_aliases={2: 0},
        scratch_shapes=[pltpu.VMEM((3 * REP_B, ATTN_TQ, kw), F32)] + _pipeline_scratch(ATTN_B_GROUP * REP_B, kw),
        compiler_params=_params("arbitrary", "arbitrary"),
        name=f"attn_b_s{seq}_l{layer}",
    )(slopes_b, sink_b, buf, proj, proj, proj)


def _route(logits):
    lane = lax.broadcasted_iota(jnp.int32, logits.shape, 1).astype(F32)
    far = float(ROUTER_LANES)
    gl = jnp.where(lane < N_GROUPS, logits, NEG_INF)
    gmax = jnp.max(gl, axis=-1, keepdims=True)
    gsum = jnp.sum(jnp.exp(gl - gmax), axis=-1, keepdims=True)
    g_gate = 1.0 / gsum
    gidx = jnp.min(jnp.where(gl == gmax, lane, far), axis=-1, keepdims=True)
    base = N_GROUPS + N_EXPERTS_PER_GROUP * gidx
    el = jnp.where((lane >= base) & (lane < base + N_EXPERTS_PER_GROUP), logits, NEG_INF)
    m1 = jnp.max(el, axis=-1, keepdims=True)
    i1 = jnp.min(jnp.where(el == m1, lane, far), axis=-1, keepdims=True)
    el2 = jnp.where(lane == i1, NEG_INF, el)
    m2 = jnp.max(el2, axis=-1, keepdims=True)
    i2 = jnp.min(jnp.where(el2 == m2, lane, far), axis=-1, keepdims=True)
    e2 = jnp.exp(m2 - m1)
    w1 = 1.0 / (1.0 + e2)
    w2 = e2 / (1.0 + e2)
    first = lane - ROUTE_W0 + base
    weights = jnp.where(first == i1, g_gate * w1, jnp.where(first == i2, g_gate * w2, 0.0))
    return jnp.where(lane == 0.0, gidx, weights)


def _outproj_kernel(a_ref, x_ref, wo_ref, gn_ref, g1_ref, sc2_ref, sh2_ref, lng_ref, lnb_ref,
                    wr_ref, br_ref, x1_ref, rt_ref, n_even, n_odd, mix_even, mix_odd):
    s = pl.program_id(0)

    @pl.when(s == 0)
    def _():
        for ref in (n_even, n_odd, mix_even, mix_odd):
            ref[...] = jnp.zeros_like(ref)

    def body(n_cur, n_prev, mix_prev, mix_prev2):
        y = DEEPNORM_ALPHA * x_ref[...] + (1.0 + g1_ref[...]) * mix_prev2[...]
        x1 = _layer_norm(y, lng_ref[...], lnb_ref[...])
        x1_ref[...] = x1
        h2 = (x1 * (1.0 + sc2_ref[...]) + sh2_ref[...]).astype(BF16)
        rt_ref[...] = _route(jnp.dot(h2, wr_ref[...], preferred_element_type=F32) + br_ref[...])

        mix = jnp.dot(n_prev[:, :D_A], wo_ref[:D_A, :], preferred_element_type=F32)
        mix_prev[...] = mix + jnp.dot(n_prev[:, D_A:], wo_ref[D_A:, :], preferred_element_type=F32)

        n_cur[:, :D_A] = _rms_norm(a_ref[:, :D_A], gn_ref[:, :D_A]).astype(BF16)
        n_cur[:, D_A:] = _rms_norm(a_ref[:, D_A:], gn_ref[:, D_A:]).astype(BF16)

    @pl.when(s % 2 == 0)
    def _():
        body(n_even, n_odd, mix_odd, mix_even)

    @pl.when(s % 2 == 1)
    def _():
        body(n_odd, n_even, mix_even, mix_odd)


def _outproj_call(tok, layer, attn, x, mod5, w_out_bf, gn, ln_g, ln_b, w_router_bf, b_router):
    n, d = x.shape
    tm = 512
    nt = n // tm
    late = lambda i: jnp.clip(i - 2, 0, nt - 1)
    const = lambda shape: pl.BlockSpec(shape, lambda i: (layer,) + (0,) * (len(shape) - 1),
                                       pipeline_mode=pl.Buffered(1))
    mod = lambda k: pl.BlockSpec((None, None, None, 1, d),
                                 lambda i: (layer, tok.batch_of_tile(late(i), tm), k, 0, 0))
    return pl.pallas_call(
        _outproj_kernel,
        grid=(nt + 2,),
        in_specs=[
            pl.BlockSpec((tm, D_A + D_B), lambda i: (jnp.minimum(i, nt - 1), 0)),
            pl.BlockSpec((tm, d), lambda i: (late(i), 0)),
            const((None, D_A + D_B, d)),
            const((None, 1, D_A + D_B)),
            mod(2), mod(4), mod(3),
            const((None, 1, d)),
            const((None, 1, d)),
            const((None, d, ROUTER_LANES)),
            const((None, 1, ROUTER_LANES)),
        ],
        out_specs=[
            pl.BlockSpec((tm, d), lambda i: (late(i), 0)),
            pl.BlockSpec((tm, ROUTER_LANES), lambda i: (late(i), 0)),
        ],
        out_shape=[
            jax.ShapeDtypeStruct((n, d), F32),
            jax.ShapeDtypeStruct((n, ROUTER_LANES), F32),
        ],
        scratch_shapes=[pltpu.VMEM((tm, D_A + D_B), BF16), pltpu.VMEM((tm, D_A + D_B), BF16),
                        pltpu.VMEM((tm, d), F32), pltpu.VMEM((tm, d), F32)],
        compiler_params=_params("arbitrary"),
        name=f"outproj_l{layer}",
    )(attn, x, w_out_bf, gn, mod5, mod5, mod5, ln_g, ln_b, w_router_bf, b_router)


def _moe_plan(route, n):
    tm = MOE_TM
    n_tiles = n // tm + N_GROUPS
    groups = jnp.arange(N_GROUPS, dtype=jnp.int32)
    gi = route[:, 0].astype(jnp.int32)
    onehot = (gi[:, None] == groups[None, :]).astype(jnp.int32)
    csum = jnp.cumsum(onehot, axis=0)
    counts = csum[-1]
    padded = (counts + tm - 1) // tm * tm
    ends = jnp.cumsum(padded)
    base = ends - padded
    rank = jnp.sum(csum * onehot, axis=1) - 1
    pos = jnp.sum(base[None, :] * onehot, axis=1) + rank
    k = jnp.arange(N_GROUPS * tm, dtype=jnp.int32)
    kg, kr = k // tm, k % tm
    pad_slot = base[kg] + counts[kg] + kr
    pad_pos = jnp.where(counts[kg] + kr < padded[kg], pad_slot, n_tiles * tm + k)
    tile_start = jnp.arange(n_tiles, dtype=jnp.int32) * tm
    tile_group = jnp.minimum(jnp.sum((tile_start[:, None] >= ends[None, :]).astype(jnp.int32), axis=1), N_GROUPS - 1)
    n_used = (ends[-1] // tm).reshape(1)
    return pos, pad_pos, tile_group, n_used, n_tiles


def _dispatch_kernel(pos_ref, pad_ref, x_ref, rt_ref, sc2_ref, sh2_ref, hs_ref, buf_a, buf_b, zero, sems, zsem):
    i = pl.program_id(0)
    n = pl.num_programs(0)
    tm, d = x_ref.shape
    half = tm // 2

    def wait_all():
        for s, buf in enumerate((buf_a, buf_b)):
            pltpu.make_async_copy(buf, hs_ref.at[pl.ds(0, half), :], sems.at[s]).wait()

    @pl.when(i >= 1)
    def _():
        wait_all()

    for s, buf in enumerate((buf_a, buf_b)):
        rows = slice(s * half, (s + 1) * half)
        buf[:, :d] = x_ref[rows, :] * (1.0 + sc2_ref[...]) + sh2_ref[...]
        buf[:, d:] = rt_ref[rows, :]
        for t in range(half):
            pltpu.make_async_copy(buf.at[pl.ds(t, 1), :], hs_ref.at[pl.ds(pos_ref[0, s * half + t], 1), :],
                                  sems.at[s]).start()

    @pl.when(i == n - 1)
    def _():
        n_pad = pad_ref.shape[0]
        zero[...] = jnp.zeros_like(zero)

        def issue_pad(t, carry):
            pltpu.make_async_copy(zero.at[pl.ds(0, 1), :], hs_ref.at[pl.ds(pad_ref[t], 1), :], zsem.at[0]).start()
            return carry

        lax.fori_loop(0, n_pad, issue_pad, 0)

        def wait_pad(t, carry):
            pltpu.make_async_copy(zero.at[pl.ds(0, 1), :], hs_ref.at[pl.ds(0, 1), :], zsem.at[0]).wait()
            return carry

        lax.fori_loop(0, n_pad, wait_pad, 0)
        wait_all()


def _dispatch_call(tok, layer, x1, route, mod5, pos, pad_pos, n_slots):
    n, d = x1.shape
    tm = 512
    w = d + ROUTER_LANES
    return pl.pallas_call(
        _dispatch_kernel,
        grid=(n // tm,),
        in_specs=[
            pl.BlockSpec((None, 1, tm), lambda i: (i, 0, 0), memory_space=pltpu.SMEM),
            pl.BlockSpec(memory_space=pltpu.SMEM),
            pl.BlockSpec((tm, d), lambda i: (i, 0)),
            pl.BlockSpec((tm, ROUTER_LANES), lambda i: (i, 0)),
            _mod_spec(tok, tm, layer, 4, d, 1),
            _mod_spec(tok, tm, layer, 3, d, 1),
        ],
        out_specs=pl.BlockSpec(memory_space=pl.ANY),
        out_shape=jax.ShapeDtypeStruct((n_slots, w), F32),
        scratch_shapes=[pltpu.VMEM((tm // 2, w), F32), pltpu.VMEM((tm // 2, w), F32), pltpu.VMEM((8, w), F32),
                        pltpu.SemaphoreType.DMA((2,)), pltpu.SemaphoreType.DMA((1,))],
        compiler_params=_params("arbitrary"),
        name=f"moe_dispatch_l{layer}",
    )(pos.reshape(n // tm, 1, tm), pad_pos, x1, route, mod5, mod5)


def _experts_kernel(tg_ref, nu_ref, hs_ref, wg_ref, wu_ref, wd_ref, ys_ref):
    del tg_ref
    d = ys_ref.shape[-1]
    f = wd_ref.shape[0] // N_EXPERTS_PER_GROUP

    @pl.when(pl.program_id(0) < nu_ref[0])
    def _():
        h = hs_ref[:, :d].astype(BF16)
        a = jnp.dot(h, wg_ref[...], preferred_element_type=F32)
        u = jnp.dot(h, wu_ref[...], preferred_element_type=F32)
        hid = (a * jax.nn.sigmoid(a)) * u
        parts = [hid[:, e * f:(e + 1) * f] * hs_ref[:, d + ROUTE_W0 + e:d + ROUTE_W0 + e + 1]
                 for e in range(N_EXPERTS_PER_GROUP)]
        ys_ref[...] = jnp.dot(jnp.concatenate(parts, axis=-1).astype(BF16), wd_ref[...],
                              preferred_element_type=F32)


def _experts_call(layer, hs, tile_group, n_used, n_tiles, wg4, wu4, wd4):
    tm = MOE_TM
    w = hs.shape[-1]
    d = wd4.shape[-1]
    ef = wg4.shape[-1]
    row = lambda i, tg, nu: (jnp.minimum(i, nu[0] - 1), 0)
    wspec = lambda shape: pl.BlockSpec(shape, lambda i, tg, nu: (layer, tg[jnp.minimum(i, nu[0] - 1)], 0, 0))
    return pl.pallas_call(
        _experts_kernel,
        grid_spec=pltpu.PrefetchScalarGridSpec(
            num_scalar_prefetch=2,
            grid=(n_tiles,),
            in_specs=[
                pl.BlockSpec((tm, w), row),
                wspec((None, None, d, ef)),
                wspec((None, None, d, ef)),
                wspec((None, None, ef, d)),
            ],
            out_specs=pl.BlockSpec((tm, d), row),
        ),
        out_shape=jax.ShapeDtypeStruct((n_tiles * tm, d), F32),
        compiler_params=_params("arbitrary"),
        name=f"moe_experts_l{layer}",
    )(tile_group, n_used, hs, wg4, wu4, wd4)


def _combine_kernel(pos_ref, posn_ref, x_ref, g2_ref, lng_ref, lnb_ref, ys_ref, *rest, n_prompt_tiles):
    if n_prompt_tiles is None:
        o_ref, buf_a, buf_b, sems = rest
    else:
        op_ref, os_ref, buf_a, buf_b, sems = rest
    i = pl.program_id(0)
    n = pl.num_programs(0)
    tm = x_ref.shape[0]
    half = tm // 2

    def issue(p_ref, t0, buf, s):
        for t in range(half):
            pltpu.make_async_copy(ys_ref.at[pl.ds(p_ref[0, t0 + t], 1), :], buf.at[pl.ds(t, 1), :], sems.at[s]).start()

    def wait(buf, s):
        pltpu.make_async_copy(ys_ref.at[pl.ds(0, half), :], buf, sems.at[s]).wait()

    def finish(rows, buf):
        y = DEEPNORM_ALPHA * x_ref[rows, :] + (1.0 + g2_ref[...]) * buf[...]
        return _layer_norm(y, lng_ref[...], lnb_ref[...])

    def store(rows, val):
        if n_prompt_tiles is None:
            o_ref[rows, :] = val
        else:
            @pl.when(i < n_prompt_tiles)
            def _():
                op_ref[rows, :] = val

            @pl.when(i >= n_prompt_tiles)
            def _():
                os_ref[rows, :] = val

    @pl.when(i == 0)
    def _():
        def first(t, carry):
            pltpu.make_async_copy(ys_ref.at[pl.ds(pos_ref[0, t], 1), :], buf_a.at[pl.ds(t, 1), :], sems.at[0]).start()
            return carry

        lax.fori_loop(0, half, first, 0, unroll=DMA_ISSUE_UNROLL)

    wait(buf_a, 0)
    issue(pos_ref, half, buf_b, 1)
    store(slice(0, half), finish(slice(0, half), buf_a))
    wait(buf_b, 1)
    issue(posn_ref, 0, buf_a, 0)
    store(slice(half, tm), finish(slice(half, tm), buf_b))

    @pl.when(i == n - 1)
    def _():
        wait(buf_a, 0)


def _combine_call(tok, layer, final, x1, ys, pos, mod5, ln_g, ln_b):
    n, d = x1.shape
    tm = 512
    nt = n // tm
    npt = tok.n_prompt // tm
    const = lambda shape: pl.BlockSpec(shape, lambda i: (layer,) + (0,) * (len(shape) - 1))
    if final:
        out_specs = [
            pl.BlockSpec((tm, d), lambda i: (jnp.minimum(i, npt - 1), 0)),
            pl.BlockSpec((tm, d), lambda i: (jnp.maximum(i - npt, 0), 0)),
        ]
        out_shape = [jax.ShapeDtypeStruct((tok.n_prompt, d), F32), jax.ShapeDtypeStruct((tok.n_sample, d), F32)]
    else:
        out_specs = pl.BlockSpec((tm, d), lambda i: (i, 0))
        out_shape = jax.ShapeDtypeStruct((n, d), F32)
    pos3 = pos.reshape(nt, 1, tm)
    return pl.pallas_call(
        functools.partial(_combine_kernel, n_prompt_tiles=npt if final else None),
        grid=(nt,),
        in_specs=[
            pl.BlockSpec((None, 1, tm), lambda i: (i, 0, 0), memory_space=pltpu.SMEM),
            pl.BlockSpec((None, 1, tm), lambda i: (jnp.minimum(i + 1, nt - 1), 0, 0), memory_space=pltpu.SMEM),
            pl.BlockSpec((tm, d), lambda i: (i, 0)),
            _mod_spec(tok, tm, layer, 5, d, 1),
            const((None, 1, d)),
            const((None, 1, d)),
            pl.BlockSpec(memory_space=pl.ANY),
        ],
        out_specs=out_specs,
        out_shape=out_shape,
        scratch_shapes=[pltpu.VMEM((tm // 2, d), F32), pltpu.VMEM((tm // 2, d), F32),
                        pltpu.SemaphoreType.DMA((2,))],
        compiler_params=_params("arbitrary"),
        name=f"moe_combine_l{layer}",
    )(pos3, pos3, x1, mod5, ln_g, ln_b, ys)


def kernel(x_prompt, x_sample, c_prompt, c_sample, ln_in_g, ln_in_b, w_ada, b_ada, w_in, sink_b, gn_a, gn_b,
           w_out, ln1_g, ln1_b, w_rg, b_rg, w_re, b_re, w_gate, w_up, w_down, ln2_g, ln2_b):
    bp, sp, d = x_prompt.shape
    bs, ss, _ = x_sample.shape
    depth = w_in.shape[0]
    tok = _Tokens(bp, sp, bs, ss)
    nb = bp + bs
    nbp = -(-nb // 8) * 8
    assert tok.n_prompt % ss == 0

    slopes_a, slopes_b = (jnp.asarray(s) for s in _alibi_slopes())
    w_in_bf = w_in.astype(BF16)
    w_out_bf = w_out.astype(BF16)
    g, e, f = N_GROUPS, N_EXPERTS_PER_GROUP, w_gate.shape[-1]
    wg4 = jnp.transpose(w_gate.astype(BF16), (0, 1, 3, 2, 4)).reshape(depth, g, d, e * f)
    wu4 = jnp.transpose(w_up.astype(BF16), (0, 1, 3, 2, 4)).reshape(depth, g, d, e * f)
    wd4 = w_down.astype(BF16).reshape(depth, g, e * f, d)
    gn = jnp.concatenate([gn_a, gn_b], axis=-1).reshape(depth, 1, D_A + D_B)
    w_router_bf = jnp.concatenate(
        [w_rg, jnp.transpose(w_re, (0, 2, 1, 3)).reshape(depth, d, g * e),
         jnp.zeros((depth, d, ROUTER_LANES - g - g * e), F32)], axis=-1).astype(BF16)
    b_router = jnp.concatenate(
        [b_rg, b_re.reshape(depth, g * e), jnp.zeros((depth, ROUTER_LANES - g - g * e), F32)],
        axis=-1).reshape(depth, 1, ROUTER_LANES)
    ln1g, ln1b = ln1_g.reshape(depth, 1, d), ln1_b.reshape(depth, 1, d)
    ln2g, ln2b = ln2_g.reshape(depth, 1, d), ln2_b.reshape(depth, 1, d)

    c_pad = jnp.concatenate([c_prompt, c_sample, jnp.zeros((nbp - nb, d), F32)], axis=0)
    mod5 = _ada_call(c_pad, w_ada, b_ada).reshape(depth, nbp, 6, 1, d)

    x = _ln_in_call(tok, x_prompt.reshape(bp * sp, d), x_sample.reshape(bs * ss, d), ln_in_g, ln_in_b)

    sample_block0 = tok.n_prompt // ss
    for layer in range(depth):
        proj = _inproj_call(tok, layer, x, mod5, w_in_bf)
        attn = _attn_a_call(None, proj, slopes_a, sp, bp, 0)
        attn = _attn_a_call(attn, proj, slopes_a, ss, bs, sample_block0)
        attn = _attn_b_call(attn, proj, slopes_b, sink_b, layer, sp, bp, 0)
        attn = _attn_b_call(attn, proj, slopes_b, sink_b, layer, ss, bs, sample_block0)
        x1, route = _outproj_call(tok, layer, attn, x, mod5, w_out_bf, gn, ln1g, ln1b, w_router_bf, b_router)
        pos, pad_pos, tile_group, n_used, n_tiles = _moe_plan(route, tok.n)
        hs = _dispatch_call(tok, layer, x1, route, mod5, pos, pad_pos, (n_tiles + N_GROUPS) * MOE_TM)
        ys = _experts_call(layer, hs, tile_group, n_used, n_tiles, wg4, wu4, wd4)
        x = _combine_call(tok, layer, layer == depth - 1, x1, ys, pos, mod5, ln2g, ln2b)

    y_prompt, y_sample = x
    return y_prompt.reshape(bp, sp, d), y_sample.reshape(bs, ss, d)
```

```python
import functools
import math

import numpy as np
import jax
import jax.numpy as jnp
from jax import lax
from jax.experimental import pallas as pl
from jax.experimental.pallas import tpu as pltpu

HEAD_DIM = 128
N_HEADS_A = 8
N_HEADS_B = 8
N_KV_B = 2
REP_B = N_HEADS_B // N_KV_B
D_A = N_HEADS_A * HEAD_DIM
D_B = N_HEADS_B * HEAD_DIM
D_KV_B = N_KV_B * HEAD_DIM
D_IN = 3 * D_A + D_B + 2 * D_KV_B
DILATED_PATTERNS = ((128, 1), (512, 4), (2048, 16))
SWA_RADIUS = 128
N_GROUPS = 4
N_EXPERTS_PER_GROUP = 4
N_EXPERTS = N_GROUPS * N_EXPERTS_PER_GROUP
MODEL_DEPTH = 4
DEEPNORM_ALPHA = (2 * MODEL_DEPTH) ** 0.25
LN_EPS = 1e-5
NEG_INF = -1e30
ATTN_SCALE = HEAD_DIM ** -0.5

ROUTER_LANES = 128
ROUTE_W0 = 4
MOE_TM = 512
DMA_ISSUE_UNROLL = 8
ATTN_TQ = 128
ATTN_A_GROUP = 4
ATTN_B_GROUP = 1
VMEM_LIMIT = 56 * 1024 * 1024

F32 = jnp.float32
BF16 = jnp.bfloat16


def _alibi_slopes():
    n = N_HEADS_A + N_HEADS_B
    s = (2.0 ** (-8.0 * np.arange(1, n + 1, dtype=np.float32) / n)).astype(np.float32)
    return s[0::2], s[1::2]


def _params(*sem):
    return pltpu.CompilerParams(dimension_semantics=sem, vmem_limit_bytes=VMEM_LIMIT)


def _layer_norm(x, g, b):
    mu = jnp.mean(x, axis=-1, keepdims=True)
    xc = x - mu
    var = jnp.mean(xc * xc, axis=-1, keepdims=True)
    return xc * lax.rsqrt(var + LN_EPS) * g + b


def _rms_norm(x, g):
    return x * lax.rsqrt(jnp.mean(x * x, axis=-1, keepdims=True) + LN_EPS) * g


def _ada_kernel(c_ref, w_ref, b_ref, o_ref):
    c = c_ref[...]
    a = (c * jax.nn.sigmoid(c)).astype(BF16)
    o_ref[...] = jnp.dot(a, w_ref[...].astype(BF16), preferred_element_type=F32) + b_ref[...]


def _ada_call(c_pad, w_ada, b_ada):
    depth, d, e = w_ada.shape
    nbp = c_pad.shape[0]
    tn = 1024
    return pl.pallas_call(
        _ada_kernel,
        grid=(depth, e // tn),
        in_specs=[
            pl.BlockSpec((nbp, d), lambda l, j: (0, 0)),
            pl.BlockSpec((None, d, tn), lambda l, j: (l, 0, j)),
            pl.BlockSpec((None, 1, tn), lambda l, j: (l, 0, j)),
        ],
        out_specs=pl.BlockSpec((None, nbp, tn), lambda l, j: (l, 0, j)),
        out_shape=jax.ShapeDtypeStruct((depth, nbp, e), F32),
        compiler_params=_params("arbitrary", "arbitrary"),
        name="ada_mod",
    )(c_pad, w_ada, b_ada.reshape(depth, 1, e))


class _Tokens:
    def __init__(self, bp, sp, bs, ss):
        self.bp, self.sp, self.bs, self.ss = bp, sp, bs, ss
        self.n_prompt = bp * sp
        self.n_sample = bs * ss
        self.n = bp * sp + bs * ss

    def batch_of_tile(self, i, tm):
        assert self.sp % tm == 0 and self.ss % tm == 0
        npt = self.n_prompt // tm
        return jnp.where(i < npt, i // (self.sp // tm), self.bp + (i - npt) // (self.ss // tm))


def _mod_spec(tok, tm, layer, k, d, n_grid):
    if n_grid == 1:
        return pl.BlockSpec((None, None, None, 1, d), lambda i: (layer, tok.batch_of_tile(i, tm), k, 0, 0))
    return pl.BlockSpec((None, None, None, 1, d), lambda i, j: (layer, tok.batch_of_tile(i, tm), k, 0, 0))


def _ln_in_kernel(xp_ref, xs_ref, g_ref, b_ref, o_ref, *, n_prompt_tiles):
    i = pl.program_id(0)

    @pl.when(i < n_prompt_tiles)
    def _():
        o_ref[...] = _layer_norm(xp_ref[...], g_ref[...], b_ref[...])

    @pl.when(i >= n_prompt_tiles)
    def _():
        o_ref[...] = _layer_norm(xs_ref[...], g_ref[...], b_ref[...])


def _ln_in_call(tok, xp, xs, g, b):
    d = xp.shape[-1]
    tm = 512
    npt = tok.n_prompt // tm
    return pl.pallas_call(
        functools.partial(_ln_in_kernel, n_prompt_tiles=npt),
        grid=(tok.n // tm,),
        in_specs=[
            pl.BlockSpec((tm, d), lambda i: (jnp.minimum(i, npt - 1), 0)),
            pl.BlockSpec((tm, d), lambda i: (jnp.maximum(i - npt, 0), 0)),
            pl.BlockSpec((1, d), lambda i: (0, 0)),
            pl.BlockSpec((1, d), lambda i: (0, 0)),
        ],
        out_specs=pl.BlockSpec((tm, d), lambda i: (i, 0)),
        out_shape=jax.ShapeDtypeStruct((tok.n, d), F32),
        compiler_params=_params("arbitrary"),
        name="ln_in",
    )(xp, xs, g.reshape(1, d), b.reshape(1, d))


def _inproj_kernel(x_ref, sh_ref, sc_ref, w_ref, o_ref, h_scr):
    @pl.when(pl.program_id(1) == 0)
    def _():
        h_scr[...] = (x_ref[...] * (1.0 + sc_ref[...]) + sh_ref[...]).astype(BF16)

    o_ref[...] = jnp.dot(h_scr[...], w_ref[...], preferred_element_type=F32).astype(o_ref.dtype)


def _inproj_call(tok, layer, x, mod5, w_in_bf):
    n, d = x.shape
    tm, tn = 1024, 1536
    return pl.pallas_call(
        _inproj_kernel,
        grid=(n // tm, D_IN // tn),
        in_specs=[
            pl.BlockSpec((tm, d), lambda i, j: (i, 0)),
            _mod_spec(tok, tm, layer, 0, d, 2),
            _mod_spec(tok, tm, layer, 1, d, 2),
            pl.BlockSpec((None, d, tn), lambda i, j: (layer, 0, j)),
        ],
        out_specs=pl.BlockSpec((tm, tn), lambda i, j: (i, j)),
        out_shape=jax.ShapeDtypeStruct((n, D_IN), BF16),
        scratch_shapes=[pltpu.VMEM((tm, d), BF16)],
        compiler_params=_params("arbitrary", "arbitrary"),
        name=f"inproj_l{layer}",
    )(x, mod5, mod5, w_in_bf)


def _combine_inproj_kernel(pos_ref, posn_ref, x1_ref, g2_ref, lng_ref, lnb_ref, sh_ref, sc_ref, w_ref, ys_ref,
                           x_ref, o_ref, h_scr, buf_even, buf_odd, sems, *, n_tiles, n_col):
    i = pl.program_id(0)
    j = pl.program_id(1)
    tm = x1_ref.shape[0]
    assert n_col >= 2 and tm % (n_col - 1) == 0
    per_step = tm // (n_col - 1)

    def wait(buf, s):
        pltpu.make_async_copy(ys_ref.at[pl.ds(0, tm), :], buf, sems.at[s]).wait()

    def project():
        o_ref[...] = jnp.dot(h_scr[...], w_ref[...], preferred_element_type=F32).astype(o_ref.dtype)

    @pl.when((i == 0) & (j == 0))
    def _():
        def first(t, carry):
            pltpu.make_async_copy(ys_ref.at[pl.ds(pos_ref[0, t], 1), :], buf_even.at[pl.ds(t, 1), :], sems.at[0]).start()
            return carry

        lax.fori_loop(0, tm, first, 0, unroll=DMA_ISSUE_UNROLL)

    def tile(buf_cur, s_cur, buf_nxt, s_nxt):
        @pl.when(j == 0)
        def _():
            wait(buf_cur, s_cur)
            y = DEEPNORM_ALPHA * x1_ref[...] + (1.0 + g2_ref[...]) * buf_cur[...]
            x = _layer_norm(y, lng_ref[...], lnb_ref[...])
            x_ref[...] = x
            h_scr[...] = (x * (1.0 + sc_ref[...]) + sh_ref[...]).astype(BF16)
            project()

        for jj in range(1, n_col):
            @pl.when(j == jj)
            def _(jj=jj):
                for t in range((jj - 1) * per_step, jj * per_step):
                    pltpu.make_async_copy(ys_ref.at[pl.ds(posn_ref[0, t], 1), :], buf_nxt.at[pl.ds(t, 1), :],
                                          sems.at[s_nxt]).start()
                project()

    @pl.when(i % 2 == 0)
    def _():
        tile(buf_even, 0, buf_odd, 1)

    @pl.when(i % 2 == 1)
    def _():
        tile(buf_odd, 1, buf_even, 0)

    @pl.when((i == n_tiles - 1) & (j == n_col - 1))
    def _():
        if (n_tiles - 1) % 2 == 0:
            wait(buf_odd, 1)
        else:
            wait(buf_even, 0)


def _combine_inproj_call(tok, layer, x1, ys, pos, mod5, ln_g, ln_b, w_in_bf):
    n, d = x1.shape
    tm, tn = 512, 1536
    nt = n // tm
    n_col = D_IN // tn
    prev = layer - 1
    pos3 = pos.reshape(nt, 1, tm)
    const = lambda shape: pl.BlockSpec(shape, lambda i, j: (prev,) + (0,) * (len(shape) - 1))
    return pl.pallas_call(
        functools.partial(_combine_inproj_kernel, n_tiles=nt, n_col=n_col),
        grid=(nt, n_col),
        in_specs=[
            pl.BlockSpec((None, 1, tm), lambda i, j: (i, 0, 0), memory_space=pltpu.SMEM),
            pl.BlockSpec((None, 1, tm), lambda i, j: (jnp.minimum(i + 1, nt - 1), 0, 0), memory_space=pltpu.SMEM),
            pl.BlockSpec((tm, d), lambda i, j: (i, 0)),
            _mod_spec(tok, tm, prev, 5, d, 2),
            const((None, 1, d)),
            const((None, 1, d)),
            _mod_spec(tok, tm, layer, 0, d, 2),
            _mod_spec(tok, tm, layer, 1, d, 2),
            pl.BlockSpec((None, d, tn), lambda i, j: (layer, 0, j)),
            pl.BlockSpec(memory_space=pl.ANY),
        ],
        out_specs=[
            pl.BlockSpec((tm, d), lambda i, j: (i, 0)),
            pl.BlockSpec((tm, tn), lambda i, j: (i, j)),
        ],
        out_shape=[
            jax.ShapeDtypeStruct((n, d), F32),
            jax.ShapeDtypeStruct((n, D_IN), BF16),
        ],
        scratch_shapes=[pltpu.VMEM((tm, d), BF16), pltpu.VMEM((tm, d), F32), pltpu.VMEM((tm, d), F32),
                        pltpu.SemaphoreType.DMA((2,))],
        compiler_params=_params("arbitrary", "arbitrary"),
        name=f"combine_inproj_l{layer}",
    )(pos3, pos3, x1, mod5, ln_g, ln_b, mod5, mod5, w_in_bf, ys)


def _band_offsets(radius, kw):
    return (0, -radius, ATTN_TQ - kw)


def _fill_bias(bias_ref, base, slope, dil, radius, kw):
    row = lax.broadcasted_iota(jnp.int32, (ATTN_TQ, kw), 0)
    col = lax.broadcasted_iota(jnp.int32, (ATTN_TQ, kw), 1)
    for v, off in enumerate(_band_offsets(radius, kw)):
        a = jnp.abs(col - row + off)
        bias = -slope * (a * dil).astype(F32)
        bias_ref[base + v] = jnp.where(a <= radius, bias, NEG_INF)


def _band_attention(n_blocks, grp, nblk, kw, radius, heads, q_of, k_ref, v_ref, bias_ref, sink_of,
                    s_scr, p_scr, st_scr, finalize):
    tq = ATTN_TQ
    assert n_blocks % grp == 0
    n_it = n_blocks // grp
    offs = _band_offsets(radius, kw)
    align = math.gcd(tq, *[abs(o) for o in offs if o])

    def geom(it, j):
        i = it * grp + j
        variant = jnp.where(i % nblk == 0, 0, jnp.where(i % nblk == nblk - 1, 2, 1))
        off = jnp.where(variant == 0, offs[0], jnp.where(variant == 1, offs[1], offs[2]))
        q0 = pl.multiple_of(i * tq, tq)
        k0 = pl.multiple_of(i * tq + off, align)
        return i, variant, q0, k0

    def qk(it):
        it = jnp.asarray(it, jnp.int32)
        slot = it % 2
        for j in range(grp):
            _, variant, q0, k0 = geom(it, j)
            kwin = k_ref[pl.ds(k0, kw), :]
            for h in range(heads):
                s = lax.dot_general(q_of(h, pl.ds(q0, tq)), kwin, (((1,), (1,)), ((), ())),
                                    preferred_element_type=F32)
                s_scr[slot, j * heads + h, :, :kw] = s * ATTN_SCALE + bias_ref[3 * h + variant]

    def sm(it):
        it = jnp.asarray(it, jnp.int32)
        slot = it % 2
        for e in range(grp * heads):
            s = s_scr[slot, e, :, :kw]
            m = jnp.max(s, axis=-1, keepdims=True)
            sink = None if sink_of is None else sink_of(e % heads)
            if sink is not None:
                m = jnp.maximum(m, sink)
            p = jnp.exp(s - m)
            l = jnp.sum(p, axis=-1, keepdims=True)
            if sink is not None:
                l = l + jnp.exp(sink - m)
            p_scr[slot, e, :, :kw] = p.astype(BF16)
            st_scr[slot, e, 0] = jnp.broadcast_to(m, (tq, HEAD_DIM))
            st_scr[slot, e, 1] = jnp.broadcast_to(l, (tq, HEAD_DIM))

    def pv(it):
        it = jnp.asarray(it, jnp.int32)
        slot = it % 2
        for j in range(grp):
            i, _, _, k0 = geom(it, j)
            vwin = v_ref[pl.ds(k0, kw), :]
            for h in range(heads):
                e = j * heads + h
                o = jnp.dot(p_scr[slot, e, :, :kw], vwin, preferred_element_type=F32)
                finalize(i, h, o, st_scr[slot, e, 0], st_scr[slot, e, 1])

    qk(0)
    sm(0)
    if n_it > 1:
        qk(1)

        def body(it, carry):
            pv(it - 1)
            sm(it)
            qk(jnp.minimum(it + 1, n_it - 1))
            return carry

        lax.fori_loop(1, n_it, body, 0)
    pv(n_it - 1)


def _pipeline_scratch(entries, kw):
    return [pltpu.VMEM((2, entries, ATTN_TQ, kw), F32), pltpu.VMEM((2, entries, ATTN_TQ, kw), BF16),
            pltpu.VMEM((2, entries, 2, ATTN_TQ, HEAD_DIM), F32)]


def _attn_a_kernel(slopes_ref, buf_ref, q_ref, k_ref, v_ref, o_ref,
                   q4, k4, v4, q16, k16, v16, tmp_a, tmp_b, acc, m_s, l_s, b1, b2, b3,
                   s_scr, p_scr, st_scr, *, seq):
    del buf_ref
    tq = ATTN_TQ
    grp = ATTN_A_GROUP
    bias_refs = (b1, b2, b3)
    geom = []
    for window, dil in DILATED_PATTERNS:
        radius = window // (2 * dil)
        length = seq // dil
        geom.append((dil, radius, length, min(tq + 2 * radius, length), length // tq))
    assert [g[0] for g in geom] == [1, 4, 16]
    len4, len16 = seq // 4, seq // 16

    @pl.when(pl.program_id(1) == 0)
    def _():
        slope = slopes_ref[pl.program_id(0)]
        for (dil, radius, length, kw, nblk), bref in zip(geom, bias_refs):
            _fill_bias(bref, 0, slope, dil, radius, kw)

    for src, dst4, dst16 in ((q_ref, q4, q16), (k_ref, k4, k16), (v_ref, v4, v16)):
        tmp_a[...] = src[...].astype(F32)
        for r in range(4):
            cls = tmp_a[pl.ds(r, len4, stride=4), :]
            tmp_b[r * len4:(r + 1) * len4, :] = cls
            dst4[r * len4:(r + 1) * len4, :] = cls.astype(BF16)
        for r in range(16):
            dst16[r * len16:(r + 1) * len16, :] = tmp_b[pl.ds((r % 4) * len4 + r // 4, len16, stride=4), :].astype(BF16)

    def merge(rows, o, m, l):
        m_old = m_s[rows, :]
        m_new = jnp.maximum(m_old, m)
        a_old = jnp.exp(m_old - m_new)
        a_new = jnp.exp(m - m_new)
        acc[rows, :] = a_old * acc[rows, :] + a_new * o
        l_s[rows, :] = a_old * l_s[rows, :] + a_new * l
        m_s[rows, :] = m_new

    dil, radius, length, kw, nblk = geom[0]

    def fin0(i, h, o, m, l):
        rows = pl.ds(pl.multiple_of(i * tq, tq), tq)
        o_ref[rows, :] = o
        tmp_a[rows, :] = m
        tmp_b[rows, :] = l

    _band_attention(nblk, grp, nblk, kw, radius, 1, lambda h, rows: q_ref[rows, :], k_ref, v_ref, b1, None,
                    s_scr, p_scr, st_scr, fin0)
    for r in range(4):
        acc[r * len4:(r + 1) * len4, :] = o_ref[pl.ds(r, len4, stride=4), :]
        m_s[r * len4:(r + 1) * len4, :] = tmp_a[pl.ds(r, len4, stride=4), :]
        l_s[r * len4:(r + 1) * len4, :] = tmp_b[pl.ds(r, len4, stride=4), :]

    dil, radius, length, kw, nblk = geom[1]

    def fin4(i, h, o, m, l):
        merge(pl.ds(pl.multiple_of(i * tq, tq), tq), o, m, l)

    _band_attention(seq // tq, grp, nblk, kw, radius, 1, lambda h, rows: q4[rows, :], k4, v4, b2, None,
                    s_scr, p_scr, st_scr, fin4)

    dil, radius, length, kw, nblk = geom[2]

    def fin16(i, h, o, m, l):
        r16 = i // nblk
        l0 = (i % nblk) * tq
        merge(pl.ds((r16 % 4) * len4 + 4 * l0 + r16 // 4, tq, stride=4), o, m, l)

    _band_attention(seq // tq, grp, nblk, kw, radius, 1, lambda h, rows: q16[rows, :], k16, v16, b3, None,
                    s_scr, p_scr, st_scr, fin16)

    for r in range(4):
        o_ref[pl.ds(r, len4, stride=4), :] = acc[r * len4:(r + 1) * len4, :] / l_s[r * len4:(r + 1) * len4, :]


def _attn_a_call(buf, proj, slopes_a, seq, n_seq, row_block0):
    n = proj.shape[0]
    assert seq % (16 * ATTN_TQ) == 0
    bias_shapes = []
    kw_max = 0
    for window, dil in DILATED_PATTERNS:
        kw = min(ATTN_TQ + window // dil, seq // dil)
        kw_max = max(kw_max, kw)
        bias_shapes.append(pltpu.VMEM((3, ATTN_TQ, kw), F32))

    def spec(col0):
        return pl.BlockSpec((seq, HEAD_DIM), lambda h, b: (row_block0 + b, col0 + h))

    if buf is None:
        buf = jnp.zeros((8, 128), F32)
        aliases = {}
    else:
        aliases = {1: 0}
    rows_bf = pltpu.VMEM((seq, HEAD_DIM), BF16)
    rows_f32 = pltpu.VMEM((seq, HEAD_DIM), F32)
    return pl.pallas_call(
        functools.partial(_attn_a_kernel, seq=seq),
        grid=(N_HEADS_A, n_seq),
        in_specs=[
            pl.BlockSpec(memory_space=pltpu.SMEM),
            pl.BlockSpec(memory_space=pl.ANY),
            spec(0), spec(N_HEADS_A), spec(2 * N_HEADS_A),
        ],
        out_specs=pl.BlockSpec((seq, HEAD_DIM), lambda h, b: (row_block0 + b, h)),
        out_shape=jax.ShapeDtypeStruct((n, D_A + D_B), F32),
        input_output_aliases=aliases,
        scratch_shapes=[rows_bf] * 6 + [rows_f32] * 5 + bias_shapes + _pipeline_scratch(ATTN_A_GROUP, kw_max),
        compiler_params=_params("arbitrary", "arbitrary"),
        name=f"attn_a_s{seq}",
    )(slopes_a, buf, proj, proj, proj)


def _attn_b_kernel(slopes_ref, sink_ref, buf_ref, q_ref, k_ref, v_ref, o_ref, bias, s_scr, p_scr, st_scr,
                   *, seq, layer):
    del buf_ref
    tq = ATTN_TQ
    kw = tq + 2 * SWA_RADIUS
    nblk = seq // tq
    g = pl.program_id(0)

    @pl.when(pl.program_id(1) == 0)
    def _():
        for hh in range(REP_B):
            _fill_bias(bias, 3 * hh, slopes_ref[g * REP_B + hh], 1, SWA_RADIUS, kw)

    def fin(i, h, o, m, l):
        o_ref[pl.ds(pl.multiple_of(i * tq, tq), tq), h * HEAD_DIM:(h + 1) * HEAD_DIM] = o / l

    _band_attention(nblk, ATTN_B_GROUP, nblk, kw, SWA_RADIUS, REP_B,
                    lambda h, rows: q_ref[rows, h * HEAD_DIM:(h + 1) * HEAD_DIM], k_ref, v_ref, bias,
                    lambda h: sink_ref[layer, g * REP_B + h], s_scr, p_scr, st_scr, fin)


def _attn_b_call(buf, proj, slopes_b, sink_b, layer, seq, n_seq, row_block0):
    n = proj.shape[0]
    kw = ATTN_TQ + 2 * SWA_RADIUS
    assert seq % ATTN_TQ == 0 and seq >= kw
    wq = REP_B * HEAD_DIM
    q_col0 = 3 * D_A // wq
    k_col0 = (3 * D_A + D_B) // HEAD_DIM
    v_col0 = (3 * D_A + D_B + D_KV_B) // HEAD_DIM
    return pl.pallas_call(
        functools.partial(_attn_b_kernel, seq=seq, layer=layer),
        grid=(N_KV_B, n_seq),
        in_specs=[
            pl.BlockSpec(memory_space=pltpu.SMEM),
            pl.BlockSpec(memory_space=pltpu.SMEM),
            pl.BlockSpec(memory_space=pl.ANY),
            pl.BlockSpec((seq, wq), lambda g, b: (row_block0 + b, q_col0 + g)),
            pl.BlockSpec((seq, HEAD_DIM), lambda g, b: (row_block0 + b, k_col0 + g)),
            pl.BlockSpec((seq, HEAD_DIM), lambda g, b: (row_block0 + b, v_col0 + g)),
        ],
        out_specs=pl.BlockSpec((seq, wq), lambda g, b: (row_block0 + b, D_A // wq + g)),
        out_shape=jax.ShapeDtypeStruct((n, D_A + D_B), F32),
        input_output_aliases={2: 0},
        scratch_shapes=[pltpu.VMEM((3 * REP_B, ATTN_TQ, kw), F32)] + _pipeline_scratch(ATTN_B_GROUP * REP_B, kw),
        compiler_params=_params("arbitrary", "arbitrary"),
        name=f"attn_b_s{seq}_l{layer}",
    )(slopes_b, sink_b, buf, proj, proj, proj)


def _route(logits):
    lane = lax.broadcasted_iota(jnp.int32, logits.shape, 1).astype(F32)
    far = float(ROUTER_LANES)
    gl = jnp.where(lane < N_GROUPS, logits, NEG_INF)
    gmax = jnp.max(gl, axis=-1, keepdims=True)
    gsum = jnp.sum(jnp.exp(gl - gmax), axis=-1, keepdims=True)
    g_gate = 1.0 / gsum
    gidx = jnp.min(jnp.where(gl == gmax, lane, far), axis=-1, keepdims=True)
    base = N_GROUPS + N_EXPERTS_PER_GROUP * gidx
    el = jnp.where((lane >= base) & (lane < base + N_EXPERTS_PER_GROUP), logits, NEG_INF)
    m1 = jnp.max(el, axis=-1, keepdims=True)
    i1 = jnp.min(jnp.where(el == m1, lane, far), axis=-1, keepdims=True)
    el2 = jnp.where(lane == i1, NEG_INF, el)
    m2 = jnp.max(el2, axis=-1, keepdims=True)
    i2 = jnp.min(jnp.where(el2 == m2, lane, far), axis=-1, keepdims=True)
    e2 = jnp.exp(m2 - m1)
    w1 = 1.0 / (1.0 + e2)
    w2 = e2 / (1.0 + e2)
    first = lane - ROUTE_W0 + base
    weights = jnp.where(first == i1, g_gate * w1, jnp.where(first == i2, g_gate * w2, 0.0))
    return jnp.where(lane == 0.0, gidx, weights)


def _outproj_kernel(a_ref, x_ref, wo_ref, gn_ref, g1_ref, sc2_ref, sh2_ref, lng_ref, lnb_ref,
                    wr_ref, br_ref, x1_ref, rt_ref, n_even, n_odd, mix_even, mix_odd):
    s = pl.program_id(0)

    @pl.when(s == 0)
    def _():
        for ref in (n_even, n_odd, mix_even, mix_odd):
            ref[...] = jnp.zeros_like(ref)

    def body(n_cur, n_prev, mix_prev, mix_prev2):
        y = DEEPNORM_ALPHA * x_ref[...] + (1.0 + g1_ref[...]) * mix_prev2[...]
        x1 = _layer_norm(y, lng_ref[...], lnb_ref[...])
        x1_ref[...] = x1
        h2 = (x1 * (1.0 + sc2_ref[...]) + sh2_ref[...]).astype(BF16)
        rt_ref[...] = _route(jnp.dot(h2, wr_ref[...], preferred_element_type=F32) + br_ref[...])

        mix = jnp.dot(n_prev[:, :D_A], wo_ref[:D_A, :], preferred_element_type=F32)
        mix_prev[...] = mix + jnp.dot(n_prev[:, D_A:], wo_ref[D_A:, :], preferred_element_type=F32)

        n_cur[:, :D_A] = _rms_norm(a_ref[:, :D_A], gn_ref[:, :D_A]).astype(BF16)
        n_cur[:, D_A:] = _rms_norm(a_ref[:, D_A:], gn_ref[:, D_A:]).astype(BF16)

    @pl.when(s % 2 == 0)
    def _():
        body(n_even, n_odd, mix_odd, mix_even)

    @pl.when(s % 2 == 1)
    def _():
        body(n_odd, n_even, mix_even, mix_odd)


def _outproj_call(tok, layer, attn, x, mod5, w_out_bf, gn, ln_g, ln_b, w_router_bf, b_router):
    n, d = x.shape
    tm = 512
    nt = n // tm
    late = lambda i: jnp.clip(i - 2, 0, nt - 1)
    const = lambda shape: pl.BlockSpec(shape, lambda i: (layer,) + (0,) * (len(shape) - 1),
                                       pipeline_mode=pl.Buffered(1))
    mod = lambda k: pl.BlockSpec((None, None, None, 1, d),
                                 lambda i: (layer, tok.batch_of_tile(late(i), tm), k, 0, 0))
    return pl.pallas_call(
        _outproj_kernel,
        grid=(nt + 2,),
        in_specs=[
            pl.BlockSpec((tm, D_A + D_B), lambda i: (jnp.minimum(i, nt - 1), 0)),
            pl.BlockSpec((tm, d), lambda i: (late(i), 0)),
            const((None, D_A + D_B, d)),
            const((None, 1, D_A + D_B)),
            mod(2), mod(4), mod(3),
            const((None, 1, d)),
            const((None, 1, d)),
            const((None, d, ROUTER_LANES)),
            const((None, 1, ROUTER_LANES)),
        ],
        out_specs=[
            pl.BlockSpec((tm, d), lambda i: (late(i), 0)),
            pl.BlockSpec((tm, ROUTER_LANES), lambda i: (late(i), 0)),
        ],
        out_shape=[
            jax.ShapeDtypeStruct((n, d), F32),
            jax.ShapeDtypeStruct((n, ROUTER_LANES), F32),
        ],
        scratch_shapes=[pltpu.VMEM((tm, D_A + D_B), BF16), pltpu.VMEM((tm, D_A + D_B), BF16),
                        pltpu.VMEM((tm, d), F32), pltpu.VMEM((tm, d), F32)],
        compiler_params=_params("arbitrary"),
        name=f"outproj_l{layer}",
    )(attn, x, w_out_bf, gn, mod5, mod5, mod5, ln_g, ln_b, w_router_bf, b_router)


def _moe_plan(route, n):
    tm = MOE_TM
    n_tiles = n // tm + N_GROUPS
    groups = jnp.arange(N_GROUPS, dtype=jnp.int32)
    gi = route[:, 0].astype(jnp.int32)
    onehot = (gi[:, None] == groups[None, :]).astype(jnp.int32)
    csum = jnp.cumsum(onehot, axis=0)
    counts = csum[-1]
    padded = (counts + tm - 1) // tm * tm
    ends = jnp.cumsum(padded)
    base = ends - padded
    rank = jnp.sum(csum * onehot, axis=1) - 1
    pos = jnp.sum(base[None, :] * onehot, axis=1) + rank
    k = jnp.arange(N_GROUPS * tm, dtype=jnp.int32)
    kg, kr = k // tm, k % tm
    pad_slot = base[kg] + counts[kg] + kr
    pad_pos = jnp.where(counts[kg] + kr < padded[kg], pad_slot, n_tiles * tm + k)
    tile_start = jnp.arange(n_tiles, dtype=jnp.int32) * tm
    tile_group = jnp.minimum(jnp.sum((tile_start[:, None] >= ends[None, :]).astype(jnp.int32), axis=1), N_GROUPS - 1)
    n_used = (ends[-1] // tm).reshape(1)
    return pos, pad_pos, tile_group, n_used, n_tiles


def _pack_bf16_pairs(hi, lo):
    hi_bits = lax.bitcast_convert_type(hi.astype(BF16).astype(F32), jnp.uint32)
    lo_bits = lax.bitcast_convert_type(lo.astype(BF16).astype(F32), jnp.uint32)
    return hi_bits | (lo_bits >> 16)


def _unpack_bf16_pairs(words):
    hi = lax.bitcast_convert_type(words & jnp.uint32(0xFFFF0000), F32).astype(BF16)
    lo = lax.bitcast_convert_type(words << 16, F32).astype(BF16)
    return hi, lo


def _dispatch_kernel(pos_ref, pad_ref, x_ref, rt_ref, sc2_ref, sh2_ref, hs_ref, buf_a, buf_b, zero, sems, zsem):
    i = pl.program_id(0)
    n = pl.num_programs(0)
    tm, d = x_ref.shape
    half = tm // 2

    def wait_all():
        for s, buf in enumerate((buf_a, buf_b)):
            pltpu.make_async_copy(buf, hs_ref.at[pl.ds(0, half), :], sems.at[s]).wait()

    @pl.when(i >= 1)
    def _():
        wait_all()

    for s, buf in enumerate((buf_a, buf_b)):
        rows = slice(s * half, (s + 1) * half)
        h = x_ref[rows, :] * (1.0 + sc2_ref[...]) + sh2_ref[...]
        buf[:, :d // 2] = _pack_bf16_pairs(h[:, :d // 2], h[:, d // 2:])
        buf[:, d // 2:] = lax.bitcast_convert_type(rt_ref[rows, :], jnp.uint32)
        for t in range(half):
            pltpu.make_async_copy(buf.at[pl.ds(t, 1), :], hs_ref.at[pl.ds(pos_ref[0, s * half + t], 1), :],
                                  sems.at[s]).start()

    @pl.when(i == n - 1)
    def _():
        n_pad = pad_ref.shape[0]
        zero[...] = jnp.zeros_like(zero)

        def issue_pad(t, carry):
            pltpu.make_async_copy(zero.at[pl.ds(0, 1), :], hs_ref.at[pl.ds(pad_ref[t], 1), :], zsem.at[0]).start()
            return carry

        lax.fori_loop(0, n_pad, issue_pad, 0)

        def wait_pad(t, carry):
            pltpu.make_async_copy(zero.at[pl.ds(0, 1), :], hs_ref.at[pl.ds(0, 1), :], zsem.at[0]).wait()
            return carry

        lax.fori_loop(0, n_pad, wait_pad, 0)
        wait_all()


def _dispatch_call(tok, layer, x1, route, mod5, pos, pad_pos, n_slots):
    n, d = x1.shape
    tm = 512
    w = d // 2 + ROUTER_LANES
    u32 = jnp.uint32
    return pl.pallas_call(
        _dispatch_kernel,
        grid=(n // tm,),
        in_specs=[
            pl.BlockSpec((None, 1, tm), lambda i: (i, 0, 0), memory_space=pltpu.SMEM),
            pl.BlockSpec(memory_space=pltpu.SMEM),
            pl.BlockSpec((tm, d), lambda i: (i, 0)),
            pl.BlockSpec((tm, ROUTER_LANES), lambda i: (i, 0)),
            _mod_spec(tok, tm, layer, 4, d, 1),
            _mod_spec(tok, tm, layer, 3, d, 1),
        ],
        out_specs=pl.BlockSpec(memory_space=pl.ANY),
        out_shape=jax.ShapeDtypeStruct((n_slots, w), u32),
        scratch_shapes=[pltpu.VMEM((tm // 2, w), u32), pltpu.VMEM((tm // 2, w), u32), pltpu.VMEM((8, w), u32),
                        pltpu.SemaphoreType.DMA((2,)), pltpu.SemaphoreType.DMA((1,))],
        compiler_params=_params("arbitrary"),
        name=f"moe_dispatch_l{layer}",
    )(pos.reshape(n // tm, 1, tm), pad_pos, x1, route, mod5, mod5)


def _experts_kernel(tg_ref, nu_ref, hs_ref, wg_ref, wu_ref, wd_ref, ys_ref):
    del tg_ref
    d = ys_ref.shape[-1]
    f = wd_ref.shape[0] // N_EXPERTS_PER_GROUP

    @pl.when(pl.program_id(0) < nu_ref[0])
    def _():
        h = jnp.concatenate(_unpack_bf16_pairs(hs_ref[:, :d // 2]), axis=-1)
        route = lax.bitcast_convert_type(hs_ref[:, d // 2:], F32)
        a = jnp.dot(h, wg_ref[...], preferred_element_type=F32)
        u = jnp.dot(h, wu_ref[...], preferred_element_type=F32)
        hid = (a * jax.nn.sigmoid(a)) * u
        parts = [hid[:, e * f:(e + 1) * f] * route[:, ROUTE_W0 + e:ROUTE_W0 + e + 1]
                 for e in range(N_EXPERTS_PER_GROUP)]
        ys_ref[...] = jnp.dot(jnp.concatenate(parts, axis=-1).astype(BF16), wd_ref[...],
                              preferred_element_type=F32)


def _experts_call(layer, hs, tile_group, n_used, n_tiles, wg4, wu4, wd4):
    tm = MOE_TM
    w = hs.shape[-1]
    d = wd4.shape[-1]
    ef = wg4.shape[-1]
    row = lambda i, tg, nu: (jnp.minimum(i, nu[0] - 1), 0)
    wspec = lambda shape: pl.BlockSpec(shape, lambda i, tg, nu: (layer, tg[jnp.minimum(i, nu[0] - 1)], 0, 0))
    return pl.pallas_call(
        _experts_kernel,
        grid_spec=pltpu.PrefetchScalarGridSpec(
            num_scalar_prefetch=2,
            grid=(n_tiles,),
            in_specs=[
                pl.BlockSpec((tm, w), row),
                wspec((None, None, d, ef)),
                wspec((None, None, d, ef)),
                wspec((None, None, ef, d)),
            ],
            out_specs=pl.BlockSpec((tm, d), row),
        ),
        out_shape=jax.ShapeDtypeStruct((n_tiles * tm, d), F32),
        compiler_params=_params("arbitrary"),
        name=f"moe_experts_l{layer}",
    )(tile_group, n_used, hs, wg4, wu4, wd4)


def _combine_kernel(pos_ref, posn_ref, x_ref, g2_ref, lng_ref, lnb_ref, ys_ref, *rest, n_prompt_tiles):
    if n_prompt_tiles is None:
        o_ref, buf, sems = rest
    else:
        op_ref, os_ref, buf, sems = rest
    i = pl.program_id(0)
    n = pl.num_programs(0)
    tm = x_ref.shape[0]
    slot = i % 2

    def gather(p_ref, s):
        def issue(t, carry):
            pltpu.make_async_copy(ys_ref.at[pl.ds(p_ref[0, t], 1), :], buf.at[s, pl.ds(t, 1), :], sems.at[s]).start()
            return carry

        lax.fori_loop(0, tm, issue, 0, unroll=DMA_ISSUE_UNROLL)

    @pl.when(i == 0)
    def _():
        gather(pos_ref, slot)

    @pl.when(i + 1 < n)
    def _():
        gather(posn_ref, 1 - slot)

    pltpu.make_async_copy(ys_ref.at[pl.ds(0, tm), :], buf.at[slot], sems.at[slot]).wait()

    def finish():
        y = DEEPNORM_ALPHA * x_ref[...] + (1.0 + g2_ref[...]) * buf[slot]
        return _layer_norm(y, lng_ref[...], lnb_ref[...])

    if n_prompt_tiles is None:
        o_ref[...] = finish()
    else:
        @pl.when(i < n_prompt_tiles)
        def _():
            op_ref[...] = finish()

        @pl.when(i >= n_prompt_tiles)
        def _():
            os_ref[...] = finish()


def _combine_call(tok, layer, final, x1, ys, pos, mod5, ln_g, ln_b):
    n, d = x1.shape
    tm = 512
    nt = n // tm
    npt = tok.n_prompt // tm
    const = lambda shape: pl.BlockSpec(shape, lambda i: (layer,) + (0,) * (len(shape) - 1))
    if final:
        out_specs = [
            pl.BlockSpec((tm, d), lambda i: (jnp.minimum(i, npt - 1), 0)),
            pl.BlockSpec((tm, d), lambda i: (jnp.maximum(i - npt, 0), 0)),
        ]
        out_shape = [jax.ShapeDtypeStruct((tok.n_prompt, d), F32), jax.ShapeDtypeStruct((tok.n_sample, d), F32)]
    else:
        out_specs = pl.BlockSpec((tm, d), lambda i: (i, 0))
        out_shape = jax.ShapeDtypeStruct((n, d), F32)
    pos3 = pos.reshape(nt, 1, tm)
    return pl.pallas_call(
        functools.partial(_combine_kernel, n_prompt_tiles=npt if final else None),
        grid=(nt,),
        in_specs=[
            pl.BlockSpec((None, 1, tm), lambda i: (i, 0, 0), memory_space=pltpu.SMEM),
            pl.BlockSpec((None, 1, tm), lambda i: (jnp.minimum(i + 1, nt - 1), 0, 0), memory_space=pltpu.SMEM),
            pl.BlockSpec((tm, d), lambda i: (i, 0)),
            _mod_spec(tok, tm, layer, 5, d, 1),
            const((None, 1, d)),
            const((None, 1, d)),
            pl.BlockSpec(memory_space=pl.ANY),
        ],
        out_specs=out_specs,
        out_shape=out_shape,
        scratch_shapes=[pltpu.VMEM((2, tm, d), F32), pltpu.SemaphoreType.DMA((2,))],
        compiler_params=_params("arbitrary"),
        name=f"moe_combine_l{layer}",
    )(pos3, pos3, x1, mod5, ln_g, ln_b, ys)


def kernel(x_prompt, x_sample, c_prompt, c_sample, ln_in_g, ln_in_b, w_ada, b_ada, w_in, sink_b, gn_a, gn_b,
           w_out, ln1_g, ln1_b, w_rg, b_rg, w_re, b_re, w_gate, w_up, w_down, ln2_g, ln2_b):
    bp, sp, d = x_prompt.shape
    bs, ss, _ = x_sample.shape
    depth = w_in.shape[0]
    tok = _Tokens(bp, sp, bs, ss)
    nb = bp + bs
    nbp = -(-nb // 8) * 8
    assert tok.n_prompt % ss == 0

    slopes_a, slopes_b = (jnp.asarray(s) for s in _alibi_slopes())
    w_in_bf = w_in.astype(BF16)
    w_out_bf = w_out.astype(BF16)
    g, e, f = N_GROUPS, N_EXPERTS_PER_GROUP, w_gate.shape[-1]
    wg4 = jnp.transpose(w_gate.astype(BF16), (0, 1, 3, 2, 4)).reshape(depth, g, d, e * f)
    wu4 = jnp.transpose(w_up.astype(BF16), (0, 1, 3, 2, 4)).reshape(depth, g, d, e * f)
    wd4 = w_down.astype(BF16).reshape(depth, g, e * f, d)
    gn = jnp.concatenate([gn_a, gn_b], axis=-1).reshape(depth, 1, D_A + D_B)
    w_router_bf = jnp.concatenate(
        [w_rg, jnp.transpose(w_re, (0, 2, 1, 3)).reshape(depth, d, g * e),
         jnp.zeros((depth, d, ROUTER_LANES - g - g * e), F32)], axis=-1).astype(BF16)
    b_router = jnp.concatenate(
        [b_rg, b_re.reshape(depth, g * e), jnp.zeros((depth, ROUTER_LANES - g - g * e), F32)],
        axis=-1).reshape(depth, 1, ROUTER_LANES)
    ln1g, ln1b = ln1_g.reshape(depth, 1, d), ln1_b.reshape(depth, 1, d)
    ln2g, ln2b = ln2_g.reshape(depth, 1, d), ln2_b.reshape(depth, 1, d)

    c_pad = jnp.concatenate([c_prompt, c_sample, jnp.zeros((nbp - nb, d), F32)], axis=0)
    mod5 = _ada_call(c_pad, w_ada, b_ada).reshape(depth, nbp, 6, 1, d)

    x = _ln_in_call(tok, x_prompt.reshape(bp * sp, d), x_sample.reshape(bs * ss, d), ln_in_g, ln_in_b)

    sample_block0 = tok.n_prompt // ss
    for layer in range(depth):
        if layer == 0:
            proj = _inproj_call(tok, layer, x, mod5, w_in_bf)
        else:
            x, proj = _combine_inproj_call(tok, layer, x1, ys, pos, mod5, ln2g, ln2b, w_in_bf)
        attn = _attn_a_call(None, proj, slopes_a, sp, bp, 0)
        attn = _attn_a_call(attn, proj, slopes_a, ss, bs, sample_block0)
        attn = _attn_b_call(attn, proj, slopes_b, sink_b, layer, sp, bp, 0)
        attn = _attn_b_call(attn, proj, slopes_b, sink_b, layer, ss, bs, sample_block0)
        x1, route = _outproj_call(tok, layer, attn, x, mod5, w_out_bf, gn, ln1g, ln1b, w_router_bf, b_router)
        pos, pad_pos, tile_group, n_used, n_tiles = _moe_plan(route, tok.n)
        hs = _dispatch_call(tok, layer, x1, route, mod5, pos, pad_pos, (n_tiles + N_GROUPS) * MOE_TM)
        ys = _experts_call(layer, hs, tile_group, n_used, n_tiles, wg4, wu4, wd4)

    y_prompt, y_sample = _combine_call(tok, depth - 1, True, x1, ys, pos, mod5, ln2g, ln2b)
    return y_prompt.reshape(bp, sp, d), y_sample.reshape(bs, ss, d)
```

```python
import functools
import math

import numpy as np
import jax
import jax.numpy as jnp
from jax import lax
from jax.experimental import pallas as pl
from jax.experimental.pallas import tpu as pltpu

HEAD_DIM = 128
N_HEADS_A = 8
N_HEADS_B = 8
N_KV_B = 2
REP_B = N_HEADS_B // N_KV_B
D_A = N_HEADS_A * HEAD_DIM
D_B = N_HEADS_B * HEAD_DIM
D_KV_B = N_KV_B * HEAD_DIM
D_IN = 3 * D_A + D_B + 2 * D_KV_B
DILATED_PATTERNS = ((128, 1), (512, 4), (2048, 16))
SWA_RADIUS = 128
N_GROUPS = 4
N_EXPERTS_PER_GROUP = 4
N_EXPERTS = N_GROUPS * N_EXPERTS_PER_GROUP
MODEL_DEPTH = 4
DEEPNORM_ALPHA = (2 * MODEL_DEPTH) ** 0.25
LN_EPS = 1e-5
NEG_INF = -1e30
ATTN_SCALE = HEAD_DIM ** -0.5

ROUTER_LANES = 128
ROUTE_W0 = 4
MOE_TM = 512
DMA_ISSUE_UNROLL = 8
ATTN_TQ = 128
ATTN_A_GROUP = 4
ATTN_B_GROUP = 1
VMEM_LIMIT = 56 * 1024 * 1024

F32 = jnp.float32
BF16 = jnp.bfloat16


def _alibi_slopes():
    n = N_HEADS_A + N_HEADS_B
    s = (2.0 ** (-8.0 * np.arange(1, n + 1, dtype=np.float32) / n)).astype(np.float32)
    return s[0::2], s[1::2]


def _params(*sem):
    return pltpu.CompilerParams(dimension_semantics=sem, vmem_limit_bytes=VMEM_LIMIT)


def _layer_norm(x, g, b):
    mu = jnp.mean(x, axis=-1, keepdims=True)
    xc = x - mu
    var = jnp.mean(xc * xc, axis=-1, keepdims=True)
    return xc * lax.rsqrt(var + LN_EPS) * g + b


def _rms_norm(x, g):
    return x * lax.rsqrt(jnp.mean(x * x, axis=-1, keepdims=True) + LN_EPS) * g


def _ada_kernel(c_ref, w_ref, b_ref, o_ref):
    c = c_ref[...]
    a = (c * jax.nn.sigmoid(c)).astype(BF16)
    o_ref[...] = jnp.dot(a, w_ref[...].astype(BF16), preferred_element_type=F32) + b_ref[...]


def _ada_call(c_pad, w_ada, b_ada):
    depth, d, e = w_ada.shape
    nbp = c_pad.shape[0]
    tn = 1024
    return pl.pallas_call(
        _ada_kernel,
        grid=(depth, e // tn),
        in_specs=[
            pl.BlockSpec((nbp, d), lambda l, j: (0, 0)),
            pl.BlockSpec((None, d, tn), lambda l, j: (l, 0, j)),
            pl.BlockSpec((None, 1, tn), lambda l, j: (l, 0, j)),
        ],
        out_specs=pl.BlockSpec((None, nbp, tn), lambda l, j: (l, 0, j)),
        out_shape=jax.ShapeDtypeStruct((depth, nbp, e), F32),
        compiler_params=_params("arbitrary", "arbitrary"),
        name="ada_mod",
    )(c_pad, w_ada, b_ada.reshape(depth, 1, e))


class _Tokens:
    def __init__(self, bp, sp, bs, ss):
        self.bp, self.sp, self.bs, self.ss = bp, sp, bs, ss
        self.n_prompt = bp * sp
        self.n_sample = bs * ss
        self.n = bp * sp + bs * ss

    def batch_of_tile(self, i, tm):
        assert self.sp % tm == 0 and self.ss % tm == 0
        npt = self.n_prompt // tm
        return jnp.where(i < npt, i // (self.sp // tm), self.bp + (i - npt) // (self.ss // tm))


def _mod_spec(tok, tm, layer, k, d, n_grid):
    if n_grid == 1:
        return pl.BlockSpec((None, None, None, 1, d), lambda i: (layer, tok.batch_of_tile(i, tm), k, 0, 0))
    return pl.BlockSpec((None, None, None, 1, d), lambda i, j: (layer, tok.batch_of_tile(i, tm), k, 0, 0))


def _ln_in_kernel(xp_ref, xs_ref, g_ref, b_ref, o_ref, *, n_prompt_tiles):
    i = pl.program_id(0)

    @pl.when(i < n_prompt_tiles)
    def _():
        o_ref[...] = _layer_norm(xp_ref[...], g_ref[...], b_ref[...])

    @pl.when(i >= n_prompt_tiles)
    def _():
        o_ref[...] = _layer_norm(xs_ref[...], g_ref[...], b_ref[...])


def _ln_in_call(tok, xp, xs, g, b):
    d = xp.shape[-1]
    tm = 512
    npt = tok.n_prompt // tm
    return pl.pallas_call(
        functools.partial(_ln_in_kernel, n_prompt_tiles=npt),
        grid=(tok.n // tm,),
        in_specs=[
            pl.BlockSpec((tm, d), lambda i: (jnp.minimum(i, npt - 1), 0)),
            pl.BlockSpec((tm, d), lambda i: (jnp.maximum(i - npt, 0), 0)),
            pl.BlockSpec((1, d), lambda i: (0, 0)),
            pl.BlockSpec((1, d), lambda i: (0, 0)),
        ],
        out_specs=pl.BlockSpec((tm, d), lambda i: (i, 0)),
        out_shape=jax.ShapeDtypeStruct((tok.n, d), F32),
        compiler_params=_params("arbitrary"),
        name="ln_in",
    )(xp, xs, g.reshape(1, d), b.reshape(1, d))


def _inproj_kernel(x_ref, sh_ref, sc_ref, w_ref, o_ref, h_scr):
    @pl.when(pl.program_id(1) == 0)
    def _():
        h_scr[...] = (x_ref[...] * (1.0 + sc_ref[...]) + sh_ref[...]).astype(BF16)

    o_ref[...] = jnp.dot(h_scr[...], w_ref[...], preferred_element_type=F32).astype(o_ref.dtype)


def _inproj_call(tok, layer, x, mod5, w_in_bf):
    n, d = x.shape
    tm, tn = 1024, 1536
    return pl.pallas_call(
        _inproj_kernel,
        grid=(n // tm, D_IN // tn),
        in_specs=[
            pl.BlockSpec((tm, d), lambda i, j: (i, 0)),
            _mod_spec(tok, tm, layer, 0, d, 2),
            _mod_spec(tok, tm, layer, 1, d, 2),
            pl.BlockSpec((None, d, tn), lambda i, j: (layer, 0, j)),
        ],
        out_specs=pl.BlockSpec((tm, tn), lambda i, j: (i, j)),
        out_shape=jax.ShapeDtypeStruct((n, D_IN), BF16),
        scratch_shapes=[pltpu.VMEM((tm, d), BF16)],
        compiler_params=_params("arbitrary", "arbitrary"),
        name=f"inproj_l{layer}",
    )(x, mod5, mod5, w_in_bf)


def _combine_inproj_kernel(pos_ref, posn_ref, x1_ref, g2_ref, lng_ref, lnb_ref, sh_ref, sc_ref, w_ref, ys_ref,
                           x_ref, o_ref, h_scr, buf_even, buf_odd, sems, *, n_tiles, n_col):
    i = pl.program_id(0)
    j = pl.program_id(1)
    tm = x1_ref.shape[0]

    def wait(buf, s):
        pltpu.make_async_copy(ys_ref.at[pl.ds(0, tm), :], buf, sems.at[s]).wait()

    def project():
        o_ref[...] = jnp.dot(h_scr[...], w_ref[...], preferred_element_type=F32).astype(o_ref.dtype)

    @pl.when((i == 0) & (j == 0))
    def _():
        def first(t, carry):
            pltpu.make_async_copy(ys_ref.at[pl.ds(pos_ref[0, t], 1), :], buf_even.at[pl.ds(t, 1), :], sems.at[0]).start()
            return carry

        lax.fori_loop(0, tm, first, 0, unroll=DMA_ISSUE_UNROLL)

    def tile(buf_cur, s_cur, buf_nxt, s_nxt):
        @pl.when(j == 0)
        def _():
            wait(buf_cur, s_cur)
            for t in range(tm):
                pltpu.make_async_copy(ys_ref.at[pl.ds(posn_ref[0, t], 1), :], buf_nxt.at[pl.ds(t, 1), :],
                                      sems.at[s_nxt]).start()
            y = DEEPNORM_ALPHA * x1_ref[...] + (1.0 + g2_ref[...]) * buf_cur[...]
            x = _layer_norm(y, lng_ref[...], lnb_ref[...])
            x_ref[...] = x
            h_scr[...] = (x * (1.0 + sc_ref[...]) + sh_ref[...]).astype(BF16)
            project()

        @pl.when(j > 0)
        def _():
            project()

    @pl.when(i % 2 == 0)
    def _():
        tile(buf_even, 0, buf_odd, 1)

    @pl.when(i % 2 == 1)
    def _():
        tile(buf_odd, 1, buf_even, 0)

    @pl.when((i == n_tiles - 1) & (j == n_col - 1))
    def _():
        if (n_tiles - 1) % 2 == 0:
            wait(buf_odd, 1)
        else:
            wait(buf_even, 0)


def _combine_inproj_call(tok, layer, x1, ys, pos, mod5, ln_g, ln_b, w_in_bf):
    n, d = x1.shape
    tm, tn = 512, 1536
    nt = n // tm
    n_col = D_IN // tn
    prev = layer - 1
    pos3 = pos.reshape(nt, 1, tm)
    const = lambda shape: pl.BlockSpec(shape, lambda i, j: (prev,) + (0,) * (len(shape) - 1))
    return pl.pallas_call(
        functools.partial(_combine_inproj_kernel, n_tiles=nt, n_col=n_col),
        grid=(nt, n_col),
        in_specs=[
            pl.BlockSpec((None, 1, tm), lambda i, j: (i, 0, 0), memory_space=pltpu.SMEM),
            pl.BlockSpec((None, 1, tm), lambda i, j: (jnp.minimum(i + 1, nt - 1), 0, 0), memory_space=pltpu.SMEM),
            pl.BlockSpec((tm, d), lambda i, j: (i, 0)),
            _mod_spec(tok, tm, prev, 5, d, 2),
            const((None, 1, d)),
            const((None, 1, d)),
            _mod_spec(tok, tm, layer, 0, d, 2),
            _mod_spec(tok, tm, layer, 1, d, 2),
            pl.BlockSpec((None, d, tn), lambda i, j: (layer, 0, j)),
            pl.BlockSpec(memory_space=pl.ANY),
        ],
        out_specs=[
            pl.BlockSpec((tm, d), lambda i, j: (i, 0)),
            pl.BlockSpec((tm, tn), lambda i, j: (i, j)),
        ],
        out_shape=[
            jax.ShapeDtypeStruct((n, d), F32),
            jax.ShapeDtypeStruct((n, D_IN), BF16),
        ],
        scratch_shapes=[pltpu.VMEM((tm, d), BF16), pltpu.VMEM((tm, d), F32), pltpu.VMEM((tm, d), F32),
                        pltpu.SemaphoreType.DMA((2,))],
        compiler_params=_params("arbitrary", "arbitrary"),
        name=f"combine_inproj_l{layer}",
    )(pos3, pos3, x1, mod5, ln_g, ln_b, mod5, mod5, w_in_bf, ys)


def _band_offsets(radius, kw):
    return (0, -radius, ATTN_TQ - kw)


def _fill_bias(bias_ref, base, slope, dil, radius, kw):
    row = lax.broadcasted_iota(jnp.int32, (ATTN_TQ, kw), 0)
    col = lax.broadcasted_iota(jnp.int32, (ATTN_TQ, kw), 1)
    for v, off in enumerate(_band_offsets(radius, kw)):
        a = jnp.abs(col - row + off)
        bias = -slope * (a * dil).astype(F32)
        bias_ref[base + v] = jnp.where(a <= radius, bias, NEG_INF)


def _band_stages(kw, heads, q_of, k_ref, v_ref, bias_ref, sink_of, s_scr, p_scr, st_scr, finalize):
    tq = ATTN_TQ

    def qk(slot, blocks):
        for j, (_, variant, q0, k0) in enumerate(blocks):
            kwin = k_ref[pl.ds(k0, kw), :]
            for h in range(heads):
                s = lax.dot_general(q_of(h, pl.ds(q0, tq)), kwin, (((1,), (1,)), ((), ())),
                                    preferred_element_type=F32)
                s_scr[slot, j * heads + h, :, :kw] = s * ATTN_SCALE + bias_ref[3 * h + variant]

    def sm(slot, blocks):
        for e in range(len(blocks) * heads):
            s = s_scr[slot, e, :, :kw]
            m = jnp.max(s, axis=-1, keepdims=True)
            sink = None if sink_of is None else sink_of(e % heads)
            if sink is not None:
                m = jnp.maximum(m, sink)
            p = jnp.exp(s - m)
            l = jnp.sum(p, axis=-1, keepdims=True)
            if sink is not None:
                l = l + jnp.exp(sink - m)
            p_scr[slot, e, :, :kw] = p.astype(BF16)
            st_scr[slot, e, 0] = jnp.broadcast_to(m, (tq, HEAD_DIM))
            st_scr[slot, e, 1] = jnp.broadcast_to(l, (tq, HEAD_DIM))

    def pv(slot, blocks):
        for j, (i, _, _, k0) in enumerate(blocks):
            vwin = v_ref[pl.ds(k0, kw), :]
            for h in range(heads):
                e = j * heads + h
                o = jnp.dot(p_scr[slot, e, :, :kw], vwin, preferred_element_type=F32)
                finalize(i, h, o, st_scr[slot, e, 0], st_scr[slot, e, 1])

    return qk, sm, pv


def _band_attention(n_blocks, grp, nblk, kw, radius, stages):
    tq = ATTN_TQ
    assert n_blocks % grp == 0
    n_it = n_blocks // grp
    offs = _band_offsets(radius, kw)
    align = math.gcd(tq, *[abs(o) for o in offs if o])
    qk, sm, pv = stages

    def group(it):
        it = jnp.asarray(it, jnp.int32)
        blocks = []
        for j in range(grp):
            i = it * grp + j
            variant = jnp.where(i % nblk == 0, 0, jnp.where(i % nblk == nblk - 1, 2, 1))
            off = jnp.where(variant == 0, offs[0], jnp.where(variant == 1, offs[1], offs[2]))
            blocks.append((i, variant, pl.multiple_of(i * tq, tq), pl.multiple_of(i * tq + off, align)))
        return it % 2, blocks

    qk(*group(0))
    sm(*group(0))
    if n_it > 1:
        qk(*group(1))

        def body(it, carry):
            pv(*group(it - 1))
            sm(*group(it))
            qk(*group(jnp.minimum(it + 1, n_it - 1)))
            return carry

        lax.fori_loop(1, n_it, body, 0)
    pv(*group(n_it - 1))


def _band_attention_static(segments):
    tq = ATTN_TQ
    sched = []
    for n_blocks, grp, nblk, kw, radius, stages, before_first_pv in segments:
        assert n_blocks % grp == 0
        offs = _band_offsets(radius, kw)
        for it in range(n_blocks // grp):
            blocks = []
            for i in range(it * grp, (it + 1) * grp):
                variant = 0 if i % nblk == 0 else (2 if i % nblk == nblk - 1 else 1)
                blocks.append((i, variant, i * tq, i * tq + offs[variant]))
            sched.append((stages, blocks, before_first_pv if it == 0 else None))

    def run_pv(t):
        (_, _, pv), blocks, before = sched[t]
        if before is not None:
            before()
        pv(t % 2, blocks)

    sched[0][0][0](0, sched[0][1])
    for t, ((_, sm, _), blocks, _) in enumerate(sched):
        if t >= 1:
            run_pv(t - 1)
        sm(t % 2, blocks)
        if t + 1 < len(sched):
            sched[t + 1][0][0]((t + 1) % 2, sched[t + 1][1])
    run_pv(len(sched) - 1)


def _pipeline_scratch(entries, kw):
    return [pltpu.VMEM((2, entries, ATTN_TQ, kw), F32), pltpu.VMEM((2, entries, ATTN_TQ, kw), BF16),
            pltpu.VMEM((2, entries, 2, ATTN_TQ, HEAD_DIM), F32)]


def _attn_a_kernel(slopes_ref, buf_ref, q_ref, k_ref, v_ref, o_ref,
                   q4, k4, v4, q16, k16, v16, tmp_a, tmp_b, acc, m_s, l_s, b1, b2, b3,
                   s_scr, p_scr, st_scr, *, seq):
    del buf_ref
    tq = ATTN_TQ
    grp = ATTN_A_GROUP
    bias_refs = (b1, b2, b3)
    geom = []
    for window, dil in DILATED_PATTERNS:
        radius = window // (2 * dil)
        length = seq // dil
        geom.append((dil, radius, length, min(tq + 2 * radius, length), length // tq))
    assert [g[0] for g in geom] == [1, 4, 16]
    len4, len16 = seq // 4, seq // 16

    @pl.when(pl.program_id(1) == 0)
    def _():
        slope = slopes_ref[pl.program_id(0)]
        for (dil, radius, length, kw, nblk), bref in zip(geom, bias_refs):
            _fill_bias(bref, 0, slope, dil, radius, kw)

    for src, dst4, dst16 in ((q_ref, q4, q16), (k_ref, k4, k16), (v_ref, v4, v16)):
        tmp_a[...] = src[...].astype(F32)
        for r in range(4):
            cls = tmp_a[pl.ds(r, len4, stride=4), :]
            tmp_b[r * len4:(r + 1) * len4, :] = cls
            dst4[r * len4:(r + 1) * len4, :] = cls.astype(BF16)
        for r in range(16):
            dst16[r * len16:(r + 1) * len16, :] = tmp_b[pl.ds((r % 4) * len4 + r // 4, len16, stride=4), :].astype(BF16)

    def merge(rows, o, m, l):
        m_old = m_s[rows, :]
        m_new = jnp.maximum(m_old, m)
        a_old = jnp.exp(m_old - m_new)
        a_new = jnp.exp(m - m_new)
        acc[rows, :] = a_old * acc[rows, :] + a_new * o
        l_s[rows, :] = a_old * l_s[rows, :] + a_new * l
        m_s[rows, :] = m_new

    def fin0(i, h, o, m, l):
        rows = pl.ds(i * tq, tq)
        o_ref[rows, :] = o
        tmp_a[rows, :] = m
        tmp_b[rows, :] = l

    def to_class_major():
        for r in range(4):
            acc[r * len4:(r + 1) * len4, :] = o_ref[pl.ds(r, len4, stride=4), :]
            m_s[r * len4:(r + 1) * len4, :] = tmp_a[pl.ds(r, len4, stride=4), :]
            l_s[r * len4:(r + 1) * len4, :] = tmp_b[pl.ds(r, len4, stride=4), :]

    def fin4(i, h, o, m, l):
        merge(pl.ds(i * tq, tq), o, m, l)

    nblk16 = geom[2][4]

    def fin16(i, h, o, m, l):
        r16 = i // nblk16
        l0 = (i % nblk16) * tq
        merge(pl.ds((r16 % 4) * len4 + 4 * l0 + r16 // 4, tq, stride=4), o, m, l)

    sources = ((q_ref, k_ref, v_ref), (q4, k4, v4), (q16, k16, v16))
    finals = (fin0, fin4, fin16)
    before = (None, to_class_major, None)
    segments = []
    for (dil, radius, length, kw, nblk), (qs, ks, vs), bref, fin, pre in zip(geom, sources, bias_refs, finals, before):
        stages = _band_stages(kw, 1, lambda h, rows, qs=qs: qs[rows, :], ks, vs, bref, None,
                              s_scr, p_scr, st_scr, fin)
        segments.append((seq // tq, grp, nblk, kw, radius, stages, pre))
    _band_attention_static(segments)

    for r in range(4):
        o_ref[pl.ds(r, len4, stride=4), :] = acc[r * len4:(r + 1) * len4, :] / l_s[r * len4:(r + 1) * len4, :]


def _attn_a_call(buf, proj, slopes_a, seq, n_seq, row_block0):
    n = proj.shape[0]
    assert seq % (16 * ATTN_TQ) == 0
    bias_shapes = []
    kw_max = 0
    for window, dil in DILATED_PATTERNS:
        kw = min(ATTN_TQ + window // dil, seq // dil)
        kw_max = max(kw_max, kw)
        bias_shapes.append(pltpu.VMEM((3, ATTN_TQ, kw), F32))

    def spec(col0):
        return pl.BlockSpec((seq, HEAD_DIM), lambda h, b: (row_block0 + b, col0 + h))

    if buf is None:
        buf = jnp.zeros((8, 128), F32)
        aliases = {}
    else:
        aliases = {1: 0}
    rows_bf = pltpu.VMEM((seq, HEAD_DIM), BF16)
    rows_f32 = pltpu.VMEM((seq, HEAD_DIM), F32)
    return pl.pallas_call(
        functools.partial(_attn_a_kernel, seq=seq),
        grid=(N_HEADS_A, n_seq),
        in_specs=[
            pl.BlockSpec(memory_space=pltpu.SMEM),
            pl.BlockSpec(memory_space=pl.ANY),
            spec(0), spec(N_HEADS_A), spec(2 * N_HEADS_A),
        ],
        out_specs=pl.BlockSpec((seq, HEAD_DIM), lambda h, b: (row_block0 + b, h)),
        out_shape=jax.ShapeDtypeStruct((n, D_A + D_B), F32),
        input_output_aliases=aliases,
        scratch_shapes=[rows_bf] * 6 + [rows_f32] * 5 + bias_shapes + _pipeline_scratch(ATTN_A_GROUP, kw_max),
        compiler_params=_params("arbitrary", "arbitrary"),
        name=f"attn_a_s{seq}",
    )(slopes_a, buf, proj, proj, proj)


def _attn_b_kernel(slopes_ref, sink_ref, buf_ref, q_ref, k_ref, v_ref, o_ref, bias, s_scr, p_scr, st_scr,
                   *, seq, layer):
    del buf_ref
    tq = ATTN_TQ
    kw = tq + 2 * SWA_RADIUS
    nblk = seq // tq
    g = pl.program_id(0)

    @pl.when(pl.program_id(1) == 0)
    def _():
        for hh in range(REP_B):
            _fill_bias(bias, 3 * hh, slopes_ref[g * REP_B + hh], 1, SWA_RADIUS, kw)

    def fin(i, h, o, m, l):
        o_ref[pl.ds(pl.multiple_of(i * tq, tq), tq), h * HEAD_DIM:(h + 1) * HEAD_DIM] = o / l

    stages = _band_stages(kw, REP_B, lambda h, rows: q_ref[rows, h * HEAD_DIM:(h + 1) * HEAD_DIM], k_ref, v_ref,
                          bias, lambda h: sink_ref[layer, g * REP_B + h], s_scr, p_scr, st_scr, fin)
    _band_attention(nblk, ATTN_B_GROUP, nblk, kw, SWA_RADIUS, stages)


def _attn_b_call(buf, proj, slopes_b, sink_b, layer, seq, n_seq, row_block0):
    n = proj.shape[0]
    kw = ATTN_TQ + 2 * SWA_RADIUS
    assert seq % ATTN_TQ == 0 and seq >= kw
    wq = REP_B * HEAD_DIM
    q_col0 = 3 * D_A // wq
    k_col0 = (3 * D_A + D_B) // HEAD_DIM
    v_col0 = (3 * D_A + D_B + D_KV_B) // HEAD_DIM
    return pl.pallas_call(
        functools.partial(_attn_b_kernel, seq=seq, layer=layer),
        grid=(N_KV_B, n_seq),
        in_specs=[
            pl.BlockSpec(memory_space=pltpu.SMEM),
            pl.BlockSpec(memory_space=pltpu.SMEM),
            pl.BlockSpec(memory_space=pl.ANY),
            pl.BlockSpec((seq, wq), lambda g, b: (row_block0 + b, q_col0 + g)),
            pl.BlockSpec((seq, HEAD_DIM), lambda g, b: (row_block0 + b, k_col0 + g)),
            pl.BlockSpec((seq, HEAD_DIM), lambda g, b: (row_block0 + b, v_col0 + g)),
        ],
        out_specs=pl.BlockSpec((seq, wq), lambda g, b: (row_block0 + b, D_A // wq + g)),
        out_shape=jax.ShapeDtypeStruct((n, D_A + D_B), F32),
        input_output_aliases={2: 0},
        scratch_shapes=[pltpu.VMEM((3 * REP_B, ATTN_TQ, kw), F32)] + _pipeline_scratch(ATTN_B_GROUP * REP_B, kw),
        compiler_params=_params("arbitrary", "arbitrary"),
        name=f"attn_b_s{seq}_l{layer}",
    )(slopes_b, sink_b, buf, proj, proj, proj)


def _route(logits):
    lane = lax.broadcasted_iota(jnp.int32, logits.shape, 1).astype(F32)
    far = float(ROUTER_LANES)
    gl = jnp.where(lane < N_GROUPS, logits, NEG_INF)
    gmax = jnp.max(gl, axis=-1, keepdims=True)
    gsum = jnp.sum(jnp.exp(gl - gmax), axis=-1, keepdims=True)
    g_gate = 1.0 / gsum
    gidx = jnp.min(jnp.where(gl == gmax, lane, far), axis=-1, keepdims=True)
    base = N_GROUPS + N_EXPERTS_PER_GROUP * gidx
    el = jnp.where((lane >= base) & (lane < base + N_EXPERTS_PER_GROUP), logits, NEG_INF)
    m1 = jnp.max(el, axis=-1, keepdims=True)
    i1 = jnp.min(jnp.where(el == m1, lane, far), axis=-1, keepdims=True)
    el2 = jnp.where(lane == i1, NEG_INF, el)
    m2 = jnp.max(el2, axis=-1, keepdims=True)
    i2 = jnp.min(jnp.where(el2 == m2, lane, far), axis=-1, keepdims=True)
    e2 = jnp.exp(m2 - m1)
    w1 = 1.0 / (1.0 + e2)
    w2 = e2 / (1.0 + e2)
    first = lane - ROUTE_W0 + base
    weights = jnp.where(first == i1, g_gate * w1, jnp.where(first == i2, g_gate * w2, 0.0))
    return jnp.where(lane == 0.0, gidx, weights)


def _outproj_kernel(a_ref, x_ref, wo_ref, gn_ref, g1_ref, sc2_ref, sh2_ref, lng_ref, lnb_ref,
                    wr_ref, br_ref, x1_ref, rt_ref, n_even, n_odd, mix_even, mix_odd):
    s = pl.program_id(0)

    @pl.when(s == 0)
    def _():
        for ref in (n_even, n_odd, mix_even, mix_odd):
            ref[...] = jnp.zeros_like(ref)

    def body(n_cur, n_prev, mix_prev, mix_prev2):
        y = DEEPNORM_ALPHA * x_ref[...] + (1.0 + g1_ref[...]) * mix_prev2[...]
        x1 = _layer_norm(y, lng_ref[...], lnb_ref[...])
        x1_ref[...] = x1
        h2 = (x1 * (1.0 + sc2_ref[...]) + sh2_ref[...]).astype(BF16)
        rt_ref[...] = _route(jnp.dot(h2, wr_ref[...], preferred_element_type=F32) + br_ref[...])

        mix = jnp.dot(n_prev[:, :D_A], wo_ref[:D_A, :], preferred_element_type=F32)
        mix_prev[...] = mix + jnp.dot(n_prev[:, D_A:], wo_ref[D_A:, :], preferred_element_type=F32)

        n_cur[:, :D_A] = _rms_norm(a_ref[:, :D_A], gn_ref[:, :D_A]).astype(BF16)
        n_cur[:, D_A:] = _rms_norm(a_ref[:, D_A:], gn_ref[:, D_A:]).astype(BF16)

    @pl.when(s % 2 == 0)
    def _():
        body(n_even, n_odd, mix_odd, mix_even)

    @pl.when(s % 2 == 1)
    def _():
        body(n_odd, n_even, mix_even, mix_odd)


def _outproj_call(tok, layer, attn, x, mod5, w_out_bf, gn, ln_g, ln_b, w_router_bf, b_router):
    n, d = x.shape
    tm = 512
    nt = n // tm
    late = lambda i: jnp.clip(i - 2, 0, nt - 1)
    const = lambda shape: pl.BlockSpec(shape, lambda i: (layer,) + (0,) * (len(shape) - 1),
                                       pipeline_mode=pl.Buffered(1))
    mod = lambda k: pl.BlockSpec((None, None, None, 1, d),
                                 lambda i: (layer, tok.batch_of_tile(late(i), tm), k, 0, 0))
    return pl.pallas_call(
        _outproj_kernel,
        grid=(nt + 2,),
        in_specs=[
            pl.BlockSpec((tm, D_A + D_B), lambda i: (jnp.minimum(i, nt - 1), 0)),
            pl.BlockSpec((tm, d), lambda i: (late(i), 0)),
            const((None, D_A + D_B, d)),
            const((None, 1, D_A + D_B)),
            mod(2), mod(4), mod(3),
            const((None, 1, d)),
            const((None, 1, d)),
            const((None, d, ROUTER_LANES)),
            const((None, 1, ROUTER_LANES)),
        ],
        out_specs=[
            pl.BlockSpec((tm, d), lambda i: (late(i), 0)),
            pl.BlockSpec((tm, ROUTER_LANES), lambda i: (late(i), 0)),
        ],
        out_shape=[
            jax.ShapeDtypeStruct((n, d), F32),
            jax.ShapeDtypeStruct((n, ROUTER_LANES), F32),
        ],
        scratch_shapes=[pltpu.VMEM((tm, D_A + D_B), BF16), pltpu.VMEM((tm, D_A + D_B), BF16),
                        pltpu.VMEM((tm, d), F32), pltpu.VMEM((tm, d), F32)],
        compiler_params=_params("arbitrary"),
        name=f"outproj_l{layer}",
    )(attn, x, w_out_bf, gn, mod5, mod5, mod5, ln_g, ln_b, w_router_bf, b_router)


def _moe_plan(route, n):
    tm = MOE_TM
    n_tiles = n // tm + N_GROUPS
    groups = jnp.arange(N_GROUPS, dtype=jnp.int32)
    gi = route[:, 0].astype(jnp.int32)
    onehot = (gi[:, None] == groups[None, :]).astype(jnp.int32)
    csum = jnp.cumsum(onehot, axis=0)
    counts = csum[-1]
    padded = (counts + tm - 1) // tm * tm
    ends = jnp.cumsum(padded)
    base = ends - padded
    rank = jnp.sum(csum * onehot, axis=1) - 1
    pos = jnp.sum(base[None, :] * onehot, axis=1) + rank
    k = jnp.arange(N_GROUPS * tm, dtype=jnp.int32)
    kg, kr = k // tm, k % tm
    pad_slot = base[kg] + counts[kg] + kr
    pad_pos = jnp.where(counts[kg] + kr < padded[kg], pad_slot, n_tiles * tm + k)
    tile_start = jnp.arange(n_tiles, dtype=jnp.int32) * tm
    tile_group = jnp.minimum(jnp.sum((tile_start[:, None] >= ends[None, :]).astype(jnp.int32), axis=1), N_GROUPS - 1)
    n_used = (ends[-1] // tm).reshape(1)
    return pos, pad_pos, tile_group, n_used, n_tiles


def _pack_bf16_pairs(hi, lo):
    hi_bits = lax.bitcast_convert_type(hi.astype(BF16).astype(F32), jnp.uint32)
    lo_bits = lax.bitcast_convert_type(lo.astype(BF16).astype(F32), jnp.uint32)
    return hi_bits | (lo_bits >> 16)


def _unpack_bf16_pairs(words):
    hi = lax.bitcast_convert_type(words & jnp.uint32(0xFFFF0000), F32).astype(BF16)
    lo = lax.bitcast_convert_type(words << 16, F32).astype(BF16)
    return hi, lo


def _dispatch_kernel(pos_ref, pad_ref, x_ref, rt_ref, sc2_ref, sh2_ref, hs_ref, buf_a, buf_b, zero, sems, zsem):
    i = pl.program_id(0)
    n = pl.num_programs(0)
    tm, d = x_ref.shape
    half = tm // 2

    def wait_all():
        for s, buf in enumerate((buf_a, buf_b)):
            pltpu.make_async_copy(buf, hs_ref.at[pl.ds(0, half), :], sems.at[s]).wait()

    for s, buf in enumerate((buf_a, buf_b)):
        rows = slice(s * half, (s + 1) * half)

        @pl.when(i >= 1)
        def _():
            pltpu.make_async_copy(buf, hs_ref.at[pl.ds(0, half), :], sems.at[s]).wait()

        h = x_ref[rows, :] * (1.0 + sc2_ref[...]) + sh2_ref[...]
        buf[:, :d // 2] = _pack_bf16_pairs(h[:, :d // 2], h[:, d // 2:])
        buf[:, d // 2:] = lax.bitcast_convert_type(rt_ref[rows, :], jnp.uint32)
        for t in range(half):
            pltpu.make_async_copy(buf.at[pl.ds(t, 1), :], hs_ref.at[pl.ds(pos_ref[0, s * half + t], 1), :],
                                  sems.at[s]).start()

    @pl.when(i == n - 1)
    def _():
        n_pad = pad_ref.shape[0]
        zero[...] = jnp.zeros_like(zero)

        def issue_pad(t, carry):
            pltpu.make_async_copy(zero.at[pl.ds(0, 1), :], hs_ref.at[pl.ds(pad_ref[t], 1), :], zsem.at[0]).start()
            return carry

        lax.fori_loop(0, n_pad, issue_pad, 0)

        def wait_pad(t, carry):
            pltpu.make_async_copy(zero.at[pl.ds(0, 1), :], hs_ref.at[pl.ds(0, 1), :], zsem.at[0]).wait()
            return carry

        lax.fori_loop(0, n_pad, wait_pad, 0)
        wait_all()


def _dispatch_call(tok, layer, x1, route, mod5, pos, pad_pos, n_slots):
    n, d = x1.shape
    tm = 512
    w = d // 2 + ROUTER_LANES
    u32 = jnp.uint32
    return pl.pallas_call(
        _dispatch_kernel,
        grid=(n // tm,),
        in_specs=[
            pl.BlockSpec((None, 1, tm), lambda i: (i, 0, 0), memory_space=pltpu.SMEM),
            pl.BlockSpec(memory_space=pltpu.SMEM),
            pl.BlockSpec((tm, d), lambda i: (i, 0)),
            pl.BlockSpec((tm, ROUTER_LANES), lambda i: (i, 0)),
            _mod_spec(tok, tm, layer, 4, d, 1),
            _mod_spec(tok, tm, layer, 3, d, 1),
        ],
        out_specs=pl.BlockSpec(memory_space=pl.ANY),
        out_shape=jax.ShapeDtypeStruct((n_slots, w), u32),
        scratch_shapes=[pltpu.VMEM((tm // 2, w), u32), pltpu.VMEM((tm // 2, w), u32), pltpu.VMEM((8, w), u32),
                        pltpu.SemaphoreType.DMA((2,)), pltpu.SemaphoreType.DMA((1,))],
        compiler_params=_params("arbitrary"),
        name=f"moe_dispatch_l{layer}",
    )(pos.reshape(n // tm, 1, tm), pad_pos, x1, route, mod5, mod5)


def _experts_kernel(tg_ref, nu_ref, hs_ref, wg_ref, wu_ref, wd_ref, ys_ref):
    del tg_ref
    d = ys_ref.shape[-1]
    f = wd_ref.shape[0] // N_EXPERTS_PER_GROUP

    @pl.when(pl.program_id(0) < nu_ref[0])
    def _():
        h = jnp.concatenate(_unpack_bf16_pairs(hs_ref[:, :d // 2]), axis=-1)
        route = lax.bitcast_convert_type(hs_ref[:, d // 2:], F32)
        a = jnp.dot(h, wg_ref[...], preferred_element_type=F32)
        u = jnp.dot(h, wu_ref[...], preferred_element_type=F32)
        hid = (a * jax.nn.sigmoid(a)) * u
        parts = [hid[:, e * f:(e + 1) * f] * route[:, ROUTE_W0 + e:ROUTE_W0 + e + 1]
                 for e in range(N_EXPERTS_PER_GROUP)]
        ys_ref[...] = jnp.dot(jnp.concatenate(parts, axis=-1).astype(BF16), wd_ref[...],
                              preferred_element_type=F32)


def _experts_call(layer, hs, tile_group, n_used, n_tiles, wg4, wu4, wd4):
    tm = MOE_TM
    w = hs.shape[-1]
    d = wd4.shape[-1]
    ef = wg4.shape[-1]
    row = lambda i, tg, nu: (jnp.minimum(i, nu[0] - 1), 0)
    wspec = lambda shape: pl.BlockSpec(shape, lambda i, tg, nu: (layer, tg[jnp.minimum(i, nu[0] - 1)], 0, 0))
    return pl.pallas_call(
        _experts_kernel,
        grid_spec=pltpu.PrefetchScalarGridSpec(
            num_scalar_prefetch=2,
            grid=(n_tiles,),
            in_specs=[
                pl.BlockSpec((tm, w), row),
                wspec((None, None, d, ef)),
                wspec((None, None, d, ef)),
                wspec((None, None, ef, d)),
            ],
            out_specs=pl.BlockSpec((tm, d), row),
        ),
        out_shape=jax.ShapeDtypeStruct((n_tiles * tm, d), F32),
        compiler_params=_params("arbitrary"),
        name=f"moe_experts_l{layer}",
    )(tile_group, n_used, hs, wg4, wu4, wd4)


def _combine_kernel(pos_ref, posn_ref, x_ref, g2_ref, lng_ref, lnb_ref, ys_ref, *rest, n_prompt_tiles):
    if n_prompt_tiles is None:
        o_ref, buf, sems = rest
    else:
        op_ref, os_ref, buf, sems = rest
    i = pl.program_id(0)
    n = pl.num_programs(0)
    tm = x_ref.shape[0]
    slot = i % 2

    def gather(p_ref, s):
        def issue(t, carry):
            pltpu.make_async_copy(ys_ref.at[pl.ds(p_ref[0, t], 1), :], buf.at[s, pl.ds(t, 1), :], sems.at[s]).start()
            return carry

        lax.fori_loop(0, tm, issue, 0, unroll=DMA_ISSUE_UNROLL)

    @pl.when(i == 0)
    def _():
        gather(pos_ref, slot)

    @pl.when(i + 1 < n)
    def _():
        gather(posn_ref, 1 - slot)

    pltpu.make_async_copy(ys_ref.at[pl.ds(0, tm), :], buf.at[slot], sems.at[slot]).wait()

    def finish():
        y = DEEPNORM_ALPHA * x_ref[...] + (1.0 + g2_ref[...]) * buf[slot]
        return _layer_norm(y, lng_ref[...], lnb_ref[...])

    if n_prompt_tiles is None:
        o_ref[...] = finish()
    else:
        @pl.when(i < n_prompt_tiles)
        def _():
            op_ref[...] = finish()

        @pl.when(i >= n_prompt_tiles)
        def _():
            os_ref[...] = finish()


def _combine_call(tok, layer, final, x1, ys, pos, mod5, ln_g, ln_b):
    n, d = x1.shape
    tm = 512
    nt = n // tm
    npt = tok.n_prompt // tm
    const = lambda shape: pl.BlockSpec(shape, lambda i: (layer,) + (0,) * (len(shape) - 1))
    if final:
        out_specs = [
            pl.BlockSpec((tm, d), lambda i: (jnp.minimum(i, npt - 1), 0)),
            pl.BlockSpec((tm, d), lambda i: (jnp.maximum(i - npt, 0), 0)),
        ]
        out_shape = [jax.ShapeDtypeStruct((tok.n_prompt, d), F32), jax.ShapeDtypeStruct((tok.n_sample, d), F32)]
    else:
        out_specs = pl.BlockSpec((tm, d), lambda i: (i, 0))
        out_shape = jax.ShapeDtypeStruct((n, d), F32)
    pos3 = pos.reshape(nt, 1, tm)
    return pl.pallas_call(
        functools.partial(_combine_kernel, n_prompt_tiles=npt if final else None),
        grid=(nt,),
        in_specs=[
            pl.BlockSpec((None, 1, tm), lambda i: (i, 0, 0), memory_space=pltpu.SMEM),
            pl.BlockSpec((None, 1, tm), lambda i: (jnp.minimum(i + 1, nt - 1), 0, 0), memory_space=pltpu.SMEM),
            pl.BlockSpec((tm, d), lambda i: (i, 0)),
            _mod_spec(tok, tm, layer, 5, d, 1),
            const((None, 1, d)),
            const((None, 1, d)),
            pl.BlockSpec(memory_space=pl.ANY),
        ],
        out_specs=out_specs,
        out_shape=out_shape,
        scratch_shapes=[pltpu.VMEM((2, tm, d), F32), pltpu.SemaphoreType.DMA((2,))],
        compiler_params=_params("arbitrary"),
        name=f"moe_combine_l{layer}",
    )(pos3, pos3, x1, mod5, ln_g, ln_b, ys)


def kernel(x_prompt, x_sample, c_prompt, c_sample, ln_in_g, ln_in_b, w_ada, b_ada, w_in, sink_b, gn_a, gn_b,
           w_out, ln1_g, ln1_b, w_rg, b_rg, w_re, b_re, w_gate, w_up, w_down, ln2_g, ln2_b):
    bp, sp, d = x_prompt.shape
    bs, ss, _ = x_sample.shape
    depth = w_in.shape[0]
    tok = _Tokens(bp, sp, bs, ss)
    nb = bp + bs
    nbp = -(-nb // 8) * 8
    assert tok.n_prompt % ss == 0

    slopes_a, slopes_b = (jnp.asarray(s) for s in _alibi_slopes())
    w_in_bf = w_in.astype(BF16)
    w_out_bf = w_out.astype(BF16)
    g, e, f = N_GROUPS, N_EXPERTS_PER_GROUP, w_gate.shape[-1]
    wg4 = jnp.transpose(w_gate.astype(BF16), (0, 1, 3, 2, 4)).reshape(depth, g, d, e * f)
    wu4 = jnp.transpose(w_up.astype(BF16), (0, 1, 3, 2, 4)).reshape(depth, g, d, e * f)
    wd4 = w_down.astype(BF16).reshape(depth, g, e * f, d)
    gn = jnp.concatenate([gn_a, gn_b], axis=-1).reshape(depth, 1, D_A + D_B)
    w_router_bf = jnp.concatenate(
        [w_rg, jnp.transpose(w_re, (0, 2, 1, 3)).reshape(depth, d, g * e),
         jnp.zeros((depth, d, ROUTER_LANES - g - g * e), F32)], axis=-1).astype(BF16)
    b_router = jnp.concatenate(
        [b_rg, b_re.reshape(depth, g * e), jnp.zeros((depth, ROUTER_LANES - g - g * e), F32)],
        axis=-1).reshape(depth, 1, ROUTER_LANES)
    ln1g, ln1b = ln1_g.reshape(depth, 1, d), ln1_b.reshape(depth, 1, d)
    ln2g, ln2b = ln2_g.reshape(depth, 1, d), ln2_b.reshape(depth, 1, d)

    c_pad = jnp.concatenate([c_prompt, c_sample, jnp.zeros((nbp - nb, d), F32)], axis=0)
    mod5 = _ada_call(c_pad, w_ada, b_ada).reshape(depth, nbp, 6, 1, d)

    x = _ln_in_call(tok, x_prompt.reshape(bp * sp, d), x_sample.reshape(bs * ss, d), ln_in_g, ln_in_b)

    sample_block0 = tok.n_prompt // ss
    for layer in range(depth):
        if layer == 0:
            proj = _inproj_call(tok, layer, x, mod5, w_in_bf)
        else:
            x, proj = _combine_inproj_call(tok, layer, x1, ys, pos, mod5, ln2g, ln2b, w_in_bf)
        attn = _attn_a_call(None, proj, slopes_a, sp, bp, 0)
        attn = _attn_a_call(attn, proj, slopes_a, ss, bs, sample_block0)
        attn = _attn_b_call(attn, proj, slopes_b, sink_b, layer, sp, bp, 0)
        attn = _attn_b_call(attn, proj, slopes_b, sink_b, layer, ss, bs, sample_block0)
        x1, route = _outproj_call(tok, layer, attn, x, mod5, w_out_bf, gn, ln1g, ln1b, w_router_bf, b_router)
        pos, pad_pos, tile_group, n_used, n_tiles = _moe_plan(route, tok.n)
        hs = _dispatch_call(tok, layer, x1, route, mod5, pos, pad_pos, (n_tiles + N_GROUPS) * MOE_TM)
        ys = _experts_call(layer, hs, tile_group, n_used, n_tiles, wg4, wu4, wd4)

    y_prompt, y_sample = _combine_call(tok, depth - 1, True, x1, ys, pos, mod5, ln2g, ln2b)
    return y_prompt.reshape(bp, sp, d), y_sample.reshape(bs, ss, d)
```

```python
import functools

import numpy as np
import jax
import jax.numpy as jnp
from jax import lax
from jax.experimental import pallas as pl
from jax.experimental.pallas import tpu as pltpu

HEAD_DIM = 128
N_HEADS_A = 8
N_HEADS_B = 8
N_KV_B = 2
REP_B = N_HEADS_B // N_KV_B
D_A = N_HEADS_A * HEAD_DIM
D_B = N_HEADS_B * HEAD_DIM
D_KV_B = N_KV_B * HEAD_DIM
D_IN = 3 * D_A + D_B + 2 * D_KV_B
DILATED_PATTERNS = ((128, 1), (512, 4), (2048, 16))
SWA_RADIUS = 128
N_GROUPS = 4
N_EXPERTS_PER_GROUP = 4
N_EXPERTS = N_GROUPS * N_EXPERTS_PER_GROUP
MODEL_DEPTH = 4
DEEPNORM_ALPHA = (2 * MODEL_DEPTH) ** 0.25
LN_EPS = 1e-5
NEG_INF = -1e30
ATTN_SCALE = HEAD_DIM ** -0.5

ROUTER_LANES = 128
ROUTE_W0 = 4
MOE_TM = 512
DMA_ISSUE_UNROLL = 8
ATTN_TQ = 128
ATTN_A_GROUP = 4
ATTN_B_GROUP = 1
VMEM_LIMIT = 56 * 1024 * 1024

F32 = jnp.float32
BF16 = jnp.bfloat16


def _alibi_slopes():
    n = N_HEADS_A + N_HEADS_B
    s = (2.0 ** (-8.0 * np.arange(1, n + 1, dtype=np.float32) / n)).astype(np.float32)
    return s[0::2], s[1::2]


def _params(*sem):
    return pltpu.CompilerParams(dimension_semantics=sem, vmem_limit_bytes=VMEM_LIMIT)


def _layer_norm(x, g, b):
    mu = jnp.mean(x, axis=-1, keepdims=True)
    xc = x - mu
    var = jnp.mean(xc * xc, axis=-1, keepdims=True)
    return xc * lax.rsqrt(var + LN_EPS) * g + b


def _rms_norm(x, g):
    return x * lax.rsqrt(jnp.mean(x * x, axis=-1, keepdims=True) + LN_EPS) * g


def _ada_kernel(c_ref, w_ref, b_ref, o_ref):
    c = c_ref[...]
    a = (c * jax.nn.sigmoid(c)).astype(BF16)
    o_ref[...] = jnp.dot(a, w_ref[...].astype(BF16), preferred_element_type=F32) + b_ref[...]


def _ada_call(c_pad, w_ada, b_ada):
    depth, d, e = w_ada.shape
    nbp = c_pad.shape[0]
    tn = 1024
    return pl.pallas_call(
        _ada_kernel,
        grid=(depth, e // tn),
        in_specs=[
            pl.BlockSpec((nbp, d), lambda l, j: (0, 0)),
            pl.BlockSpec((None, d, tn), lambda l, j: (l, 0, j)),
            pl.BlockSpec((None, 1, tn), lambda l, j: (l, 0, j)),
        ],
        out_specs=pl.BlockSpec((None, nbp, tn), lambda l, j: (l, 0, j)),
        out_shape=jax.ShapeDtypeStruct((depth, nbp, e), F32),
        compiler_params=_params("arbitrary", "arbitrary"),
        name="ada_mod",
    )(c_pad, w_ada, b_ada.reshape(depth, 1, e))


class _Tokens:
    def __init__(self, bp, sp, bs, ss):
        self.bp, self.sp, self.bs, self.ss = bp, sp, bs, ss
        self.n_prompt = bp * sp
        self.n_sample = bs * ss
        self.n = bp * sp + bs * ss

    def batch_of_tile(self, i, tm):
        assert self.sp % tm == 0 and self.ss % tm == 0
        npt = self.n_prompt // tm
        return jnp.where(i < npt, i // (self.sp // tm), self.bp + (i - npt) // (self.ss // tm))


def _mod_spec(tok, tm, layer, k, d, n_grid):
    if n_grid == 1:
        return pl.BlockSpec((None, None, None, 1, d), lambda i: (layer, tok.batch_of_tile(i, tm), k, 0, 0))
    return pl.BlockSpec((None, None, None, 1, d), lambda i, j: (layer, tok.batch_of_tile(i, tm), k, 0, 0))


def _ln_in_kernel(xp_ref, xs_ref, g_ref, b_ref, o_ref, *, n_prompt_tiles):
    i = pl.program_id(0)

    @pl.when(i < n_prompt_tiles)
    def _():
        o_ref[...] = _layer_norm(xp_ref[...], g_ref[...], b_ref[...])

    @pl.when(i >= n_prompt_tiles)
    def _():
        o_ref[...] = _layer_norm(xs_ref[...], g_ref[...], b_ref[...])


def _ln_in_call(tok, xp, xs, g, b):
    d = xp.shape[-1]
    tm = 512
    npt = tok.n_prompt // tm
    return pl.pallas_call(
        functools.partial(_ln_in_kernel, n_prompt_tiles=npt),
        grid=(tok.n // tm,),
        in_specs=[
            pl.BlockSpec((tm, d), lambda i: (jnp.minimum(i, npt - 1), 0)),
            pl.BlockSpec((tm, d), lambda i: (jnp.maximum(i - npt, 0), 0)),
            pl.BlockSpec((1, d), lambda i: (0, 0)),
            pl.BlockSpec((1, d), lambda i: (0, 0)),
        ],
        out_specs=pl.BlockSpec((tm, d), lambda i: (i, 0)),
        out_shape=jax.ShapeDtypeStruct((tok.n, d), F32),
        compiler_params=_params("arbitrary"),
        name="ln_in",
    )(xp, xs, g.reshape(1, d), b.reshape(1, d))


def _inproj_kernel(x_ref, sh_ref, sc_ref, w_ref, o_ref, h_scr):
    @pl.when(pl.program_id(1) == 0)
    def _():
        h_scr[...] = (x_ref[...] * (1.0 + sc_ref[...]) + sh_ref[...]).astype(BF16)

    o_ref[...] = jnp.dot(h_scr[...], w_ref[...], preferred_element_type=F32).astype(o_ref.dtype)


def _inproj_call(tok, layer, x, mod5, w_in_bf):
    n, d = x.shape
    tm, tn = 1024, 1536
    return pl.pallas_call(
        _inproj_kernel,
        grid=(n // tm, D_IN // tn),
        in_specs=[
            pl.BlockSpec((tm, d), lambda i, j: (i, 0)),
            _mod_spec(tok, tm, layer, 0, d, 2),
            _mod_spec(tok, tm, layer, 1, d, 2),
            pl.BlockSpec((None, d, tn), lambda i, j: (layer, 0, j)),
        ],
        out_specs=pl.BlockSpec((tm, tn), lambda i, j: (i, j)),
        out_shape=jax.ShapeDtypeStruct((n, D_IN), BF16),
        scratch_shapes=[pltpu.VMEM((tm, d), BF16)],
        compiler_params=_params("arbitrary", "arbitrary"),
        name=f"inproj_l{layer}",
    )(x, mod5, mod5, w_in_bf)


def _combine_inproj_kernel(pos_ref, posn_ref, x1_ref, g2_ref, lng_ref, lnb_ref, sh_ref, sc_ref, w_ref, ys_ref,
                           x_ref, o_ref, h_scr, buf_even, buf_odd, sems, *, n_tiles, n_col):
    i = pl.program_id(0)
    j = pl.program_id(1)
    tm = x1_ref.shape[0]

    def wait(buf, s):
        pltpu.make_async_copy(ys_ref.at[pl.ds(0, tm), :], buf, sems.at[s]).wait()

    def project():
        o_ref[...] = jnp.dot(h_scr[...], w_ref[...], preferred_element_type=F32).astype(o_ref.dtype)

    @pl.when((i == 0) & (j == 0))
    def _():
        def first(t, carry):
            pltpu.make_async_copy(ys_ref.at[pl.ds(pos_ref[0, t], 1), :], buf_even.at[pl.ds(t, 1), :], sems.at[0]).start()
            return carry

        lax.fori_loop(0, tm, first, 0, unroll=DMA_ISSUE_UNROLL)

    def tile(buf_cur, s_cur, buf_nxt, s_nxt):
        @pl.when(j == 0)
        def _():
            wait(buf_cur, s_cur)
            for t in range(tm):
                pltpu.make_async_copy(ys_ref.at[pl.ds(posn_ref[0, t], 1), :], buf_nxt.at[pl.ds(t, 1), :],
                                      sems.at[s_nxt]).start(priority=1)
            y = DEEPNORM_ALPHA * x1_ref[...] + (1.0 + g2_ref[...]) * buf_cur[...]
            x = _layer_norm(y, lng_ref[...], lnb_ref[...])
            x_ref[...] = x
            h_scr[...] = (x * (1.0 + sc_ref[...]) + sh_ref[...]).astype(BF16)
            project()

        @pl.when(j > 0)
        def _():
            project()

    @pl.when(i % 2 == 0)
    def _():
        tile(buf_even, 0, buf_odd, 1)

    @pl.when(i % 2 == 1)
    def _():
        tile(buf_odd, 1, buf_even, 0)

    @pl.when((i == n_tiles - 1) & (j == n_col - 1))
    def _():
        if (n_tiles - 1) % 2 == 0:
            wait(buf_odd, 1)
        else:
            wait(buf_even, 0)


def _combine_inproj_call(tok, layer, x1, ys, pos, mod5, ln_g, ln_b, w_in_bf):
    n, d = x1.shape
    tm, tn = 512, 1536
    nt = n // tm
    n_col = D_IN // tn
    prev = layer - 1
    pos3 = pos.reshape(nt, 1, tm)
    const = lambda shape: pl.BlockSpec(shape, lambda i, j: (prev,) + (0,) * (len(shape) - 1))
    return pl.pallas_call(
        functools.partial(_combine_inproj_kernel, n_tiles=nt, n_col=n_col),
        grid=(nt, n_col),
        in_specs=[
            pl.BlockSpec((None, 1, tm), lambda i, j: (i, 0, 0), memory_space=pltpu.SMEM),
            pl.BlockSpec((None, 1, tm), lambda i, j: (jnp.minimum(i + 1, nt - 1), 0, 0), memory_space=pltpu.SMEM),
            pl.BlockSpec((tm, d), lambda i, j: (i, 0)),
            _mod_spec(tok, tm, prev, 5, d, 2),
            const((None, 1, d)),
            const((None, 1, d)),
            _mod_spec(tok, tm, layer, 0, d, 2),
            _mod_spec(tok, tm, layer, 1, d, 2),
            pl.BlockSpec((None, d, tn), lambda i, j: (layer, 0, j)),
            pl.BlockSpec(memory_space=pl.ANY),
        ],
        out_specs=[
            pl.BlockSpec((tm, d), lambda i, j: (i, 0)),
            pl.BlockSpec((tm, tn), lambda i, j: (i, j)),
        ],
        out_shape=[
            jax.ShapeDtypeStruct((n, d), F32),
            jax.ShapeDtypeStruct((n, D_IN), BF16),
        ],
        scratch_shapes=[pltpu.VMEM((tm, d), BF16), pltpu.VMEM((tm, d), F32), pltpu.VMEM((tm, d), F32),
                        pltpu.SemaphoreType.DMA((2,))],
        compiler_params=_params("arbitrary", "arbitrary"),
        name=f"combine_inproj_l{layer}",
    )(pos3, pos3, x1, mod5, ln_g, ln_b, mod5, mod5, w_in_bf, ys)


def _band_offsets(radius, kw):
    return (0, -radius, ATTN_TQ - kw)


def _fill_bias(bias_ref, base, slope, dil, radius, kw):
    row = lax.broadcasted_iota(jnp.int32, (ATTN_TQ, kw), 0)
    col = lax.broadcasted_iota(jnp.int32, (ATTN_TQ, kw), 1)
    for v, off in enumerate(_band_offsets(radius, kw)):
        a = jnp.abs(col - row + off)
        bias = -slope * (a * dil).astype(F32)
        bias_ref[base + v] = jnp.where(a <= radius, bias, NEG_INF)


def _band_stages(kw, heads, q_of, k_ref, v_ref, bias_ref, sink_of, s_scr, p_scr, st_scr, finalize):
    tq = ATTN_TQ

    def qk(slot, blocks):
        for j, (_, variant, q0, k0) in enumerate(blocks):
            kwin = k_ref[pl.ds(k0, kw), :]
            for h in range(heads):
                s = lax.dot_general(q_of(h, pl.ds(q0, tq)), kwin, (((1,), (1,)), ((), ())),
                                    preferred_element_type=F32)
                s_scr[slot, j * heads + h, :, :kw] = s * ATTN_SCALE + bias_ref[3 * h + variant]

    def sm(slot, blocks):
        for e in range(len(blocks) * heads):
            s = s_scr[slot, e, :, :kw]
            m = jnp.max(s, axis=-1, keepdims=True)
            sink = None if sink_of is None else sink_of(e % heads)
            if sink is not None:
                m = jnp.maximum(m, sink)
            p = jnp.exp(s - m)
            l = jnp.sum(p, axis=-1, keepdims=True)
            if sink is not None:
                l = l + jnp.exp(sink - m)
            p_scr[slot, e, :, :kw] = p.astype(BF16)
            st_scr[slot, e, 0] = jnp.broadcast_to(m, (tq, HEAD_DIM))
            st_scr[slot, e, 1] = jnp.broadcast_to(l, (tq, HEAD_DIM))

    def pv(slot, blocks):
        for j, (i, _, _, k0) in enumerate(blocks):
            vwin = v_ref[pl.ds(k0, kw), :]
            for h in range(heads):
                e = j * heads + h
                o = jnp.dot(p_scr[slot, e, :, :kw], vwin, preferred_element_type=F32)
                finalize(i, h, o, st_scr[slot, e, 0], st_scr[slot, e, 1])

    return qk, sm, pv


def _band_attention_static(segments):
    tq = ATTN_TQ
    sched = []
    for n_blocks, grp, nblk, kw, radius, stages, before_first_pv in segments:
        assert n_blocks % grp == 0
        offs = _band_offsets(radius, kw)
        for it in range(n_blocks // grp):
            blocks = []
            for i in range(it * grp, (it + 1) * grp):
                variant = 0 if i % nblk == 0 else (2 if i % nblk == nblk - 1 else 1)
                blocks.append((i, variant, i * tq, i * tq + offs[variant]))
            sched.append((stages, blocks, before_first_pv if it == 0 else None))

    def run_pv(t):
        (_, _, pv), blocks, before = sched[t]
        if before is not None:
            before()
        pv(t % 2, blocks)

    sched[0][0][0](0, sched[0][1])
    for t, ((_, sm, _), blocks, _) in enumerate(sched):
        if t >= 1:
            run_pv(t - 1)
        sm(t % 2, blocks)
        if t + 1 < len(sched):
            sched[t + 1][0][0]((t + 1) % 2, sched[t + 1][1])
    run_pv(len(sched) - 1)


def _pipeline_scratch(entries, kw):
    return [pltpu.VMEM((2, entries, ATTN_TQ, kw), F32), pltpu.VMEM((2, entries, ATTN_TQ, kw), BF16),
            pltpu.VMEM((2, entries, 2, ATTN_TQ, HEAD_DIM), F32)]


def _attn_a_kernel(slopes_ref, buf_ref, q_ref, k_ref, v_ref, o_ref,
                   q4, k4, v4, q16, k16, v16, tmp_a, tmp_b, acc, m_s, l_s, b1, b2, b3,
                   s_scr, p_scr, st_scr, *, seq):
    del buf_ref
    tq = ATTN_TQ
    grp = ATTN_A_GROUP
    bias_refs = (b1, b2, b3)
    geom = []
    for window, dil in DILATED_PATTERNS:
        radius = window // (2 * dil)
        length = seq // dil
        geom.append((dil, radius, length, min(tq + 2 * radius, length), length // tq))
    assert [g[0] for g in geom] == [1, 4, 16]
    len4, len16 = seq // 4, seq // 16

    @pl.when(pl.program_id(1) == 0)
    def _():
        slope = slopes_ref[pl.program_id(0)]
        for (dil, radius, length, kw, nblk), bref in zip(geom, bias_refs):
            _fill_bias(bref, 0, slope, dil, radius, kw)

    for src, dst4, dst16 in ((q_ref, q4, q16), (k_ref, k4, k16), (v_ref, v4, v16)):
        tmp_a[...] = src[...].astype(F32)
        for r in range(4):
            cls = tmp_a[pl.ds(r, len4, stride=4), :]
            tmp_b[r * len4:(r + 1) * len4, :] = cls
            dst4[r * len4:(r + 1) * len4, :] = cls.astype(BF16)
        for r in range(16):
            dst16[r * len16:(r + 1) * len16, :] = tmp_b[pl.ds((r % 4) * len4 + r // 4, len16, stride=4), :].astype(BF16)

    def merge(rows, o, m, l):
        m_old = m_s[rows, :]
        m_new = jnp.maximum(m_old, m)
        a_old = jnp.exp(m_old - m_new)
        a_new = jnp.exp(m - m_new)
        acc[rows, :] = a_old * acc[rows, :] + a_new * o
        l_s[rows, :] = a_old * l_s[rows, :] + a_new * l
        m_s[rows, :] = m_new

    def fin0(i, h, o, m, l):
        rows = pl.ds(i * tq, tq)
        o_ref[rows, :] = o
        tmp_a[rows, :] = m
        tmp_b[rows, :] = l

    def to_class_major():
        for r in range(4):
            acc[r * len4:(r + 1) * len4, :] = o_ref[pl.ds(r, len4, stride=4), :]
            m_s[r * len4:(r + 1) * len4, :] = tmp_a[pl.ds(r, len4, stride=4), :]
            l_s[r * len4:(r + 1) * len4, :] = tmp_b[pl.ds(r, len4, stride=4), :]

    def fin4(i, h, o, m, l):
        merge(pl.ds(i * tq, tq), o, m, l)

    nblk16 = geom[2][4]

    def fin16(i, h, o, m, l):
        r16 = i // nblk16
        l0 = (i % nblk16) * tq
        merge(pl.ds((r16 % 4) * len4 + 4 * l0 + r16 // 4, tq, stride=4), o, m, l)

    sources = ((q_ref, k_ref, v_ref), (q4, k4, v4), (q16, k16, v16))
    finals = (fin0, fin4, fin16)
    before = (None, to_class_major, None)
    segments = []
    for (dil, radius, length, kw, nblk), (qs, ks, vs), bref, fin, pre in zip(geom, sources, bias_refs, finals, before):
        stages = _band_stages(kw, 1, lambda h, rows, qs=qs: qs[rows, :], ks, vs, bref, None,
                              s_scr, p_scr, st_scr, fin)
        segments.append((seq // tq, grp, nblk, kw, radius, stages, pre))
    _band_attention_static(segments)

    for r in range(4):
        o_ref[pl.ds(r, len4, stride=4), :] = acc[r * len4:(r + 1) * len4, :] / l_s[r * len4:(r + 1) * len4, :]


def _attn_a_call(buf, proj, slopes_a, seq, n_seq, row_block0):
    n = proj.shape[0]
    assert seq % (16 * ATTN_TQ) == 0
    bias_shapes = []
    kw_max = 0
    for window, dil in DILATED_PATTERNS:
        kw = min(ATTN_TQ + window // dil, seq // dil)
        kw_max = max(kw_max, kw)
        bias_shapes.append(pltpu.VMEM((3, ATTN_TQ, kw), F32))

    def spec(col0):
        return pl.BlockSpec((seq, HEAD_DIM), lambda h, b: (row_block0 + b, col0 + h))

    if buf is None:
        buf = jnp.zeros((8, 128), F32)
        aliases = {}
    else:
        aliases = {1: 0}
    rows_bf = pltpu.VMEM((seq, HEAD_DIM), BF16)
    rows_f32 = pltpu.VMEM((seq, HEAD_DIM), F32)
    return pl.pallas_call(
        functools.partial(_attn_a_kernel, seq=seq),
        grid=(N_HEADS_A, n_seq),
        in_specs=[
            pl.BlockSpec(memory_space=pltpu.SMEM),
            pl.BlockSpec(memory_space=pl.ANY),
            spec(0), spec(N_HEADS_A), spec(2 * N_HEADS_A),
        ],
        out_specs=pl.BlockSpec((seq, HEAD_DIM), lambda h, b: (row_block0 + b, h)),
        out_shape=jax.ShapeDtypeStruct((n, D_A + D_B), F32),
        input_output_aliases=aliases,
        scratch_shapes=[rows_bf] * 6 + [rows_f32] * 5 + bias_shapes + _pipeline_scratch(ATTN_A_GROUP, kw_max),
        compiler_params=_params("arbitrary", "arbitrary"),
        name=f"attn_a_s{seq}",
    )(slopes_a, buf, proj, proj, proj)


def _attn_b_kernel(slopes_ref, sink_ref, buf_ref, q_ref, k_ref, v_ref, o_ref, bias, s_scr, p_scr, st_scr,
                   *, seq, layer):
    del buf_ref
    tq = ATTN_TQ
    kw = tq + 2 * SWA_RADIUS
    nblk = seq // tq
    g = pl.program_id(0)

    @pl.when(pl.program_id(1) == 0)
    def _():
        for hh in range(REP_B):
            _fill_bias(bias, 3 * hh, slopes_ref[g * REP_B + hh], 1, SWA_RADIUS, kw)

    def fin(i, h, o, m, l):
        o_ref[pl.ds(i * tq, tq), h * HEAD_DIM:(h + 1) * HEAD_DIM] = o / l

    stages = _band_stages(kw, REP_B, lambda h, rows: q_ref[rows, h * HEAD_DIM:(h + 1) * HEAD_DIM], k_ref, v_ref,
                          bias, lambda h: sink_ref[layer, g * REP_B + h], s_scr, p_scr, st_scr, fin)
    _band_attention_static([(nblk, ATTN_B_GROUP, nblk, kw, SWA_RADIUS, stages, None)])


def _attn_b_call(buf, proj, slopes_b, sink_b, layer, seq, n_seq, row_block0):
    n = proj.shape[0]
    kw = ATTN_TQ + 2 * SWA_RADIUS
    assert seq % ATTN_TQ == 0 and seq >= kw
    wq = REP_B * HEAD_DIM
    q_col0 = 3 * D_A // wq
    k_col0 = (3 * D_A + D_B) // HEAD_DIM
    v_col0 = (3 * D_A + D_B + D_KV_B) // HEAD_DIM
    return pl.pallas_call(
        functools.partial(_attn_b_kernel, seq=seq, layer=layer),
        grid=(N_KV_B, n_seq),
        in_specs=[
            pl.BlockSpec(memory_space=pltpu.SMEM),
            pl.BlockSpec(memory_space=pltpu.SMEM),
            pl.BlockSpec(memory_space=pl.ANY),
            pl.BlockSpec((seq, wq), lambda g, b: (row_block0 + b, q_col0 + g)),
            pl.BlockSpec((seq, HEAD_DIM), lambda g, b: (row_block0 + b, k_col0 + g)),
            pl.BlockSpec((seq, HEAD_DIM), lambda g, b: (row_block0 + b, v_col0 + g)),
        ],
        out_specs=pl.BlockSpec((seq, wq), lambda g, b: (row_block0 + b, D_A // wq + g)),
        out_shape=jax.ShapeDtypeStruct((n, D_A + D_B), F32),
        input_output_aliases={2: 0},
        scratch_shapes=[pltpu.VMEM((3 * REP_B, ATTN_TQ, kw), F32)] + _pipeline_scratch(ATTN_B_GROUP * REP_B, kw),
        compiler_params=_params("arbitrary", "arbitrary"),
        name=f"attn_b_s{seq}_l{layer}",
    )(slopes_b, sink_b, buf, proj, proj, proj)


def _route(logits):
    lane = lax.broadcasted_iota(jnp.int32, logits.shape, 1).astype(F32)
    far = float(ROUTER_LANES)
    gl = jnp.where(lane < N_GROUPS, logits, NEG_INF)
    gmax = jnp.max(gl, axis=-1, keepdims=True)
    gsum = jnp.sum(jnp.exp(gl - gmax), axis=-1, keepdims=True)
    g_gate = 1.0 / gsum
    gidx = jnp.min(jnp.where(gl == gmax, lane, far), axis=-1, keepdims=True)
    base = N_GROUPS + N_EXPERTS_PER_GROUP * gidx
    el = jnp.where((lane >= base) & (lane < base + N_EXPERTS_PER_GROUP), logits, NEG_INF)
    m1 = jnp.max(el, axis=-1, keepdims=True)
    i1 = jnp.min(jnp.where(el == m1, lane, far), axis=-1, keepdims=True)
    el2 = jnp.where(lane == i1, NEG_INF, el)
    m2 = jnp.max(el2, axis=-1, keepdims=True)
    i2 = jnp.min(jnp.where(el2 == m2, lane, far), axis=-1, keepdims=True)
    e2 = jnp.exp(m2 - m1)
    w1 = 1.0 / (1.0 + e2)
    w2 = e2 / (1.0 + e2)
    first = lane - ROUTE_W0 + base
    weights = jnp.where(first == i1, g_gate * w1, jnp.where(first == i2, g_gate * w2, 0.0))
    return jnp.where(lane == 0.0, gidx, weights)


def _outproj_kernel(a_ref, x_ref, wo_ref, gn_ref, g1_ref, sc2_ref, sh2_ref, lng_ref, lnb_ref,
                    wr_ref, br_ref, x1_ref, rt_ref, n_even, n_odd, mix_even, mix_odd):
    s = pl.program_id(0)

    @pl.when(s == 0)
    def _():
        for ref in (n_even, n_odd, mix_even, mix_odd):
            ref[...] = jnp.zeros_like(ref)

    def body(n_cur, n_prev, mix_prev, mix_prev2):
        y = DEEPNORM_ALPHA * x_ref[...] + (1.0 + g1_ref[...]) * mix_prev2[...]
        x1 = _layer_norm(y, lng_ref[...], lnb_ref[...])
        x1_ref[...] = x1
        h2 = (x1 * (1.0 + sc2_ref[...]) + sh2_ref[...]).astype(BF16)
        rt_ref[...] = _route(jnp.dot(h2, wr_ref[...], preferred_element_type=F32) + br_ref[...])

        mix = jnp.dot(n_prev[:, :D_A], wo_ref[:D_A, :], preferred_element_type=F32)
        mix_prev[...] = mix + jnp.dot(n_prev[:, D_A:], wo_ref[D_A:, :], preferred_element_type=F32)

        n_cur[:, :D_A] = _rms_norm(a_ref[:, :D_A], gn_ref[:, :D_A]).astype(BF16)
        n_cur[:, D_A:] = _rms_norm(a_ref[:, D_A:], gn_ref[:, D_A:]).astype(BF16)

    @pl.when(s % 2 == 0)
    def _():
        body(n_even, n_odd, mix_odd, mix_even)

    @pl.when(s % 2 == 1)
    def _():
        body(n_odd, n_even, mix_even, mix_odd)


def _outproj_call(tok, layer, attn, x, mod5, w_out_bf, gn, ln_g, ln_b, w_router_bf, b_router):
    n, d = x.shape
    tm = 512
    nt = n // tm
    late = lambda i: jnp.clip(i - 2, 0, nt - 1)
    const = lambda shape: pl.BlockSpec(shape, lambda i: (layer,) + (0,) * (len(shape) - 1),
                                       pipeline_mode=pl.Buffered(1))
    mod = lambda k: pl.BlockSpec((None, None, None, 1, d),
                                 lambda i: (layer, tok.batch_of_tile(late(i), tm), k, 0, 0))
    return pl.pallas_call(
        _outproj_kernel,
        grid=(nt + 2,),
        in_specs=[
            pl.BlockSpec((tm, D_A + D_B), lambda i: (jnp.minimum(i, nt - 1), 0)),
            pl.BlockSpec((tm, d), lambda i: (late(i), 0)),
            const((None, D_A + D_B, d)),
            const((None, 1, D_A + D_B)),
            mod(2), mod(4), mod(3),
            const((None, 1, d)),
            const((None, 1, d)),
            const((None, d, ROUTER_LANES)),
            const((None, 1, ROUTER_LANES)),
        ],
        out_specs=[
            pl.BlockSpec((tm, d), lambda i: (late(i), 0)),
            pl.BlockSpec((tm, ROUTER_LANES), lambda i: (late(i), 0)),
        ],
        out_shape=[
            jax.ShapeDtypeStruct((n, d), F32),
            jax.ShapeDtypeStruct((n, ROUTER_LANES), F32),
        ],
        scratch_shapes=[pltpu.VMEM((tm, D_A + D_B), BF16), pltpu.VMEM((tm, D_A + D_B), BF16),
                        pltpu.VMEM((tm, d), F32), pltpu.VMEM((tm, d), F32)],
        compiler_params=_params("arbitrary"),
        name=f"outproj_l{layer}",
    )(attn, x, w_out_bf, gn, mod5, mod5, mod5, ln_g, ln_b, w_router_bf, b_router)


def _moe_plan(route, n):
    tm = MOE_TM
    n_tiles = n // tm + N_GROUPS
    groups = jnp.arange(N_GROUPS, dtype=jnp.int32)
    gi = route[:, 0].astype(jnp.int32)
    onehot = (gi[:, None] == groups[None, :]).astype(jnp.int32)
    csum = jnp.cumsum(onehot, axis=0)
    counts = csum[-1]
    padded = (counts + tm - 1) // tm * tm
    ends = jnp.cumsum(padded)
    base = ends - padded
    rank = jnp.sum(csum * onehot, axis=1) - 1
    pos = jnp.sum(base[None, :] * onehot, axis=1) + rank
    k = jnp.arange(N_GROUPS * tm, dtype=jnp.int32)
    kg, kr = k // tm, k % tm
    pad_slot = base[kg] + counts[kg] + kr
    pad_pos = jnp.where(counts[kg] + kr < padded[kg], pad_slot, n_tiles * tm + k)
    tile_start = jnp.arange(n_tiles, dtype=jnp.int32) * tm
    tile_group = jnp.minimum(jnp.sum((tile_start[:, None] >= ends[None, :]).astype(jnp.int32), axis=1), N_GROUPS - 1)
    n_used = (ends[-1] // tm).reshape(1)
    return pos, pad_pos, tile_group, n_used, n_tiles


def _pack_bf16_pairs(hi, lo):
    hi_bits = lax.bitcast_convert_type(hi.astype(BF16).astype(F32), jnp.uint32)
    lo_bits = lax.bitcast_convert_type(lo.astype(BF16).astype(F32), jnp.uint32)
    return hi_bits | (lo_bits >> 16)


def _unpack_bf16_pairs(words):
    hi = lax.bitcast_convert_type(words & jnp.uint32(0xFFFF0000), F32).astype(BF16)
    lo = lax.bitcast_convert_type(words << 16, F32).astype(BF16)
    return hi, lo


def _dispatch_kernel(pos_ref, pad_ref, x_ref, rt_ref, sc2_ref, sh2_ref, hs_ref, buf_a, buf_b, zero, sems, zsem):
    i = pl.program_id(0)
    n = pl.num_programs(0)
    tm, d = x_ref.shape
    half = tm // 2

    def wait_all():
        for s, buf in enumerate((buf_a, buf_b)):
            pltpu.make_async_copy(buf, hs_ref.at[pl.ds(0, half), :], sems.at[s]).wait()

    for s, buf in enumerate((buf_a, buf_b)):
        rows = slice(s * half, (s + 1) * half)

        @pl.when(i >= 1)
        def _():
            pltpu.make_async_copy(buf, hs_ref.at[pl.ds(0, half), :], sems.at[s]).wait()

        h = x_ref[rows, :] * (1.0 + sc2_ref[...]) + sh2_ref[...]
        buf[:, :d // 2] = _pack_bf16_pairs(h[:, :d // 2], h[:, d // 2:])
        buf[:, d // 2:] = lax.bitcast_convert_type(rt_ref[rows, :], jnp.uint32)
        for t in range(half):
            pltpu.make_async_copy(buf.at[pl.ds(t, 1), :], hs_ref.at[pl.ds(pos_ref[0, s * half + t], 1), :],
                                  sems.at[s]).start()

    @pl.when(i == n - 1)
    def _():
        n_pad = pad_ref.shape[0]
        zero[...] = jnp.zeros_like(zero)

        def issue_pad(t, carry):
            pltpu.make_async_copy(zero.at[pl.ds(0, 1), :], hs_ref.at[pl.ds(pad_ref[t], 1), :], zsem.at[0]).start()
            return carry

        lax.fori_loop(0, n_pad, issue_pad, 0)

        def wait_pad(t, carry):
            pltpu.make_async_copy(zero.at[pl.ds(0, 1), :], hs_ref.at[pl.ds(0, 1), :], zsem.at[0]).wait()
            return carry

        lax.fori_loop(0, n_pad, wait_pad, 0)
        wait_all()


def _dispatch_call(tok, layer, x1, route, mod5, pos, pad_pos, n_slots):
    n, d = x1.shape
    tm = 512
    w = d // 2 + ROUTER_LANES
    u32 = jnp.uint32
    return pl.pallas_call(
        _dispatch_kernel,
        grid=(n // tm,),
        in_specs=[
            pl.BlockSpec((None, 1, tm), lambda i: (i, 0, 0), memory_space=pltpu.SMEM),
            pl.BlockSpec(memory_space=pltpu.SMEM),
            pl.BlockSpec((tm, d), lambda i: (i, 0)),
            pl.BlockSpec((tm, ROUTER_LANES), lambda i: (i, 0)),
            _mod_spec(tok, tm, layer, 4, d, 1),
            _mod_spec(tok, tm, layer, 3, d, 1),
        ],
        out_specs=pl.BlockSpec(memory_space=pl.ANY),
        out_shape=jax.ShapeDtypeStruct((n_slots, w), u32),
        scratch_shapes=[pltpu.VMEM((tm // 2, w), u32), pltpu.VMEM((tm // 2, w), u32), pltpu.VMEM((8, w), u32),
                        pltpu.SemaphoreType.DMA((2,)), pltpu.SemaphoreType.DMA((1,))],
        compiler_params=_params("arbitrary"),
        name=f"moe_dispatch_l{layer}",
    )(pos.reshape(n // tm, 1, tm), pad_pos, x1, route, mod5, mod5)


def _experts_kernel(tg_ref, nu_ref, hs_ref, wg_ref, wu_ref, wd_ref, ys_ref):
    del tg_ref
    d = ys_ref.shape[-1]
    f = wd_ref.shape[0] // N_EXPERTS_PER_GROUP

    @pl.when(pl.program_id(0) < nu_ref[0])
    def _():
        h = jnp.concatenate(_unpack_bf16_pairs(hs_ref[:, :d // 2]), axis=-1)
        route = lax.bitcast_convert_type(hs_ref[:, d // 2:], F32)
        a = jnp.dot(h, wg_ref[...], preferred_element_type=F32)
        u = jnp.dot(h, wu_ref[...], preferred_element_type=F32)
        hid = (a * jax.nn.sigmoid(a)) * u
        parts = [hid[:, e * f:(e + 1) * f] * route[:, ROUTE_W0 + e:ROUTE_W0 + e + 1]
                 for e in range(N_EXPERTS_PER_GROUP)]
        ys_ref[...] = jnp.dot(jnp.concatenate(parts, axis=-1).astype(BF16), wd_ref[...],
                              preferred_element_type=F32)


def _experts_call(layer, hs, tile_group, n_used, n_tiles, wg4, wu4, wd4):
    tm = MOE_TM
    w = hs.shape[-1]
    d = wd4.shape[-1]
    ef = wg4.shape[-1]
    row = lambda i, tg, nu: (jnp.minimum(i, nu[0] - 1), 0)
    wspec = lambda shape: pl.BlockSpec(shape, lambda i, tg, nu: (layer, tg[jnp.minimum(i, nu[0] - 1)], 0, 0))
    return pl.pallas_call(
        _experts_kernel,
        grid_spec=pltpu.PrefetchScalarGridSpec(
            num_scalar_prefetch=2,
            grid=(n_tiles,),
            in_specs=[
                pl.BlockSpec((tm, w), row),
                wspec((None, None, d, ef)),
                wspec((None, None, d, ef)),
                wspec((None, None, ef, d)),
            ],
            out_specs=pl.BlockSpec((tm, d), row),
        ),
        out_shape=jax.ShapeDtypeStruct((n_tiles * tm, d), F32),
        compiler_params=_params("arbitrary"),
        name=f"moe_experts_l{layer}",
    )(tile_group, n_used, hs, wg4, wu4, wd4)


def _combine_kernel(pos_ref, posn_ref, x_ref, g2_ref, lng_ref, lnb_ref, ys_ref, *rest, n_prompt_tiles):
    if n_prompt_tiles is None:
        o_ref, buf, sems = rest
    else:
        op_ref, os_ref, buf, sems = rest
    i = pl.program_id(0)
    n = pl.num_programs(0)
    tm = x_ref.shape[0]
    slot = i % 2

    def gather(p_ref, s):
        def issue(t, carry):
            pltpu.make_async_copy(ys_ref.at[pl.ds(p_ref[0, t], 1), :], buf.at[s, pl.ds(t, 1), :], sems.at[s]).start()
            return carry

        lax.fori_loop(0, tm, issue, 0, unroll=DMA_ISSUE_UNROLL)

    @pl.when(i == 0)
    def _():
        gather(pos_ref, slot)

    @pl.when(i + 1 < n)
    def _():
        gather(posn_ref, 1 - slot)

    pltpu.make_async_copy(ys_ref.at[pl.ds(0, tm), :], buf.at[slot], sems.at[slot]).wait()

    def finish():
        y = DEEPNORM_ALPHA * x_ref[...] + (1.0 + g2_ref[...]) * buf[slot]
        return _layer_norm(y, lng_ref[...], lnb_ref[...])

    if n_prompt_tiles is None:
        o_ref[...] = finish()
    else:
        @pl.when(i < n_prompt_tiles)
        def _():
            op_ref[...] = finish()

        @pl.when(i >= n_prompt_tiles)
        def _():
            os_ref[...] = finish()


def _combine_call(tok, layer, final, x1, ys, pos, mod5, ln_g, ln_b):
    n, d = x1.shape
    tm = 512
    nt = n // tm
    npt = tok.n_prompt // tm
    const = lambda shape: pl.BlockSpec(shape, lambda i: (layer,) + (0,) * (len(shape) - 1))
    if final:
        out_specs = [
            pl.BlockSpec((tm, d), lambda i: (jnp.minimum(i, npt - 1), 0)),
            pl.BlockSpec((tm, d), lambda i: (jnp.maximum(i - npt, 0), 0)),
        ]
        out_shape = [jax.ShapeDtypeStruct((tok.n_prompt, d), F32), jax.ShapeDtypeStruct((tok.n_sample, d), F32)]
    else:
        out_specs = pl.BlockSpec((tm, d), lambda i: (i, 0))
        out_shape = jax.ShapeDtypeStruct((n, d), F32)
    pos3 = pos.reshape(nt, 1, tm)
    return pl.pallas_call(
        functools.partial(_combine_kernel, n_prompt_tiles=npt if final else None),
        grid=(nt,),
        in_specs=[
            pl.BlockSpec((None, 1, tm), lambda i: (i, 0, 0), memory_space=pltpu.SMEM),
            pl.BlockSpec((None, 1, tm), lambda i: (jnp.minimum(i + 1, nt - 1), 0, 0), memory_space=pltpu.SMEM),
            pl.BlockSpec((tm, d), lambda i: (i, 0)),
            _mod_spec(tok, tm, layer, 5, d, 1),
            const((None, 1, d)),
            const((None, 1, d)),
            pl.BlockSpec(memory_space=pl.ANY),
        ],
        out_specs=out_specs,
        out_shape=out_shape,
        scratch_shapes=[pltpu.VMEM((2, tm, d), F32), pltpu.SemaphoreType.DMA((2,))],
        compiler_params=_params("arbitrary"),
        name=f"moe_combine_l{layer}",
    )(pos3, pos3, x1, mod5, ln_g, ln_b, ys)


def kernel(x_prompt, x_sample, c_prompt, c_sample, ln_in_g, ln_in_b, w_ada, b_ada, w_in, sink_b, gn_a, gn_b,
           w_out, ln1_g, ln1_b, w_rg, b_rg, w_re, b_re, w_gate, w_up, w_down, ln2_g, ln2_b):
    bp, sp, d = x_prompt.shape
    bs, ss, _ = x_sample.shape
    depth = w_in.shape[0]
    tok = _Tokens(bp, sp, bs, ss)
    nb = bp + bs
    nbp = -(-nb // 8) * 8
    assert tok.n_prompt % ss == 0

    slopes_a, slopes_b = (jnp.asarray(s) for s in _alibi_slopes())
    w_in_bf = w_in.astype(BF16)
    w_out_bf = w_out.astype(BF16)
    g, e, f = N_GROUPS, N_EXPERTS_PER_GROUP, w_gate.shape[-1]
    wg4 = jnp.transpose(w_gate.astype(BF16), (0, 1, 3, 2, 4)).reshape(depth, g, d, e * f)
    wu4 = jnp.transpose(w_up.astype(BF16), (0, 1, 3, 2, 4)).reshape(depth, g, d, e * f)
    wd4 = w_down.astype(BF16).reshape(depth, g, e * f, d)
    gn = jnp.concatenate([gn_a, gn_b], axis=-1).reshape(depth, 1, D_A + D_B)
    w_router_bf = jnp.concatenate(
        [w_rg, jnp.transpose(w_re, (0, 2, 1, 3)).reshape(depth, d, g * e),
         jnp.zeros((depth, d, ROUTER_LANES - g - g * e), F32)], axis=-1).astype(BF16)
    b_router = jnp.concatenate(
        [b_rg, b_re.reshape(depth, g * e), jnp.zeros((depth, ROUTER_LANES - g - g * e), F32)],
        axis=-1).reshape(depth, 1, ROUTER_LANES)
    ln1g, ln1b = ln1_g.reshape(depth, 1, d), ln1_b.reshape(depth, 1, d)
    ln2g, ln2b = ln2_g.reshape(depth, 1, d), ln2_b.reshape(depth, 1, d)

    c_pad = jnp.concatenate([c_prompt, c_sample, jnp.zeros((nbp - nb, d), F32)], axis=0)
    mod5 = _ada_call(c_pad, w_ada, b_ada).reshape(depth, nbp, 6, 1, d)

    x = _ln_in_call(tok, x_prompt.reshape(bp * sp, d), x_sample.reshape(bs * ss, d), ln_in_g, ln_in_b)

    sample_block0 = tok.n_prompt // ss
    for layer in range(depth):
        if layer == 0:
            proj = _inproj_call(tok, layer, x, mod5, w_in_bf)
        else:
            x, proj = _combine_inproj_call(tok, layer, x1, ys, pos, mod5, ln2g, ln2b, w_in_bf)
        attn = _attn_a_call(None, proj, slopes_a, sp, bp, 0)
        attn = _attn_a_call(attn, proj, slopes_a, ss, bs, sample_block0)
        attn = _attn_b_call(attn, proj, slopes_b, sink_b, layer, sp, bp, 0)
        attn = _attn_b_call(attn, proj, slopes_b, sink_b, layer, ss, bs, sample_block0)
        x1, route = _outproj_call(tok, layer, attn, x, mod5, w_out_bf, gn, ln1g, ln1b, w_router_bf, b_router)
        pos, pad_pos, tile_group, n_used, n_tiles = _moe_plan(route, tok.n)
        hs = _dispatch_call(tok, layer, x1, route, mod5, pos, pad_pos, (n_tiles + N_GROUPS) * MOE_TM)
        ys = _experts_call(layer, hs, tile_group, n_used, n_tiles, wg4, wu4, wd4)

    y_prompt, y_sample = _combine_call(tok, depth - 1, True, x1, ys, pos, mod5, ln2g, ln2b)
    return y_prompt.reshape(bp, sp, d), y_sample.reshape(bs, ss, d)
```

```python
import functools

import numpy as np
import jax
import jax.numpy as jnp
from jax import lax
from jax.experimental import pallas as pl
from jax.experimental.pallas import tpu as pltpu

HEAD_DIM = 128
N_HEADS_A = 8
N_HEADS_B = 8
N_KV_B = 2
REP_B = N_HEADS_B // N_KV_B
D_A = N_HEADS_A * HEAD_DIM
D_B = N_HEADS_B * HEAD_DIM
D_KV_B = N_KV_B * HEAD_DIM
D_IN = 3 * D_A + D_B + 2 * D_KV_B
DILATED_PATTERNS = ((128, 1), (512, 4), (2048, 16))
SWA_RADIUS = 128
N_GROUPS = 4
N_EXPERTS_PER_GROUP = 4
N_EXPERTS = N_GROUPS * N_EXPERTS_PER_GROUP
MODEL_DEPTH = 4
DEEPNORM_ALPHA = (2 * MODEL_DEPTH) ** 0.25
LN_EPS = 1e-5
NEG_INF = -1e30
ATTN_SCALE = HEAD_DIM ** -0.5

ROUTER_LANES = 128
ROUTE_W0 = 4
MOE_TM = 512
DMA_ISSUE_UNROLL = 8
ATTN_TQ = 128
ATTN_A_GROUP = 4
ATTN_B_GROUP = 1
VMEM_LIMIT = 56 * 1024 * 1024

F32 = jnp.float32
BF16 = jnp.bfloat16


def _alibi_slopes():
    n = N_HEADS_A + N_HEADS_B
    s = (2.0 ** (-8.0 * np.arange(1, n + 1, dtype=np.float32) / n)).astype(np.float32)
    return s[0::2], s[1::2]


def _params(*sem):
    return pltpu.CompilerParams(dimension_semantics=sem, vmem_limit_bytes=VMEM_LIMIT)


def _layer_norm(x, g, b):
    mu = jnp.mean(x, axis=-1, keepdims=True)
    xc = x - mu
    var = jnp.mean(xc * xc, axis=-1, keepdims=True)
    return xc * lax.rsqrt(var + LN_EPS) * g + b


def _rms_norm(x, g):
    return x * lax.rsqrt(jnp.mean(x * x, axis=-1, keepdims=True) + LN_EPS) * g


def _ada_kernel(c_ref, w_ref, b_ref, o_ref):
    c = c_ref[...]
    a = (c * jax.nn.sigmoid(c)).astype(BF16)
    o_ref[...] = jnp.dot(a, w_ref[...].astype(BF16), preferred_element_type=F32) + b_ref[...]


def _ada_call(c_pad, w_ada, b_ada):
    depth, d, e = w_ada.shape
    nbp = c_pad.shape[0]
    tn = 1024
    return pl.pallas_call(
        _ada_kernel,
        grid=(depth, e // tn),
        in_specs=[
            pl.BlockSpec((nbp, d), lambda l, j: (0, 0)),
            pl.BlockSpec((None, d, tn), lambda l, j: (l, 0, j)),
            pl.BlockSpec((None, 1, tn), lambda l, j: (l, 0, j)),
        ],
        out_specs=pl.BlockSpec((None, nbp, tn), lambda l, j: (l, 0, j)),
        out_shape=jax.ShapeDtypeStruct((depth, nbp, e), F32),
        compiler_params=_params("arbitrary", "arbitrary"),
        name="ada_mod",
    )(c_pad, w_ada, b_ada.reshape(depth, 1, e))


class _Tokens:
    def __init__(self, bp, sp, bs, ss):
        self.bp, self.sp, self.bs, self.ss = bp, sp, bs, ss
        self.n_prompt = bp * sp
        self.n_sample = bs * ss
        self.n = bp * sp + bs * ss

    def batch_of_tile(self, i, tm):
        assert self.sp % tm == 0 and self.ss % tm == 0
        npt = self.n_prompt // tm
        return jnp.where(i < npt, i // (self.sp // tm), self.bp + (i - npt) // (self.ss // tm))


def _mod_spec(tok, tm, layer, k, d, n_grid):
    if n_grid == 1:
        return pl.BlockSpec((None, None, None, 1, d), lambda i: (layer, tok.batch_of_tile(i, tm), k, 0, 0))
    return pl.BlockSpec((None, None, None, 1, d), lambda i, j: (layer, tok.batch_of_tile(i, tm), k, 0, 0))


def _ln_in_kernel(xp_ref, xs_ref, g_ref, b_ref, o_ref, *, n_prompt_tiles):
    i = pl.program_id(0)

    @pl.when(i < n_prompt_tiles)
    def _():
        o_ref[...] = _layer_norm(xp_ref[...], g_ref[...], b_ref[...])

    @pl.when(i >= n_prompt_tiles)
    def _():
        o_ref[...] = _layer_norm(xs_ref[...], g_ref[...], b_ref[...])


def _ln_in_call(tok, xp, xs, g, b):
    d = xp.shape[-1]
    tm = 512
    npt = tok.n_prompt // tm
    return pl.pallas_call(
        functools.partial(_ln_in_kernel, n_prompt_tiles=npt),
        grid=(tok.n // tm,),
        in_specs=[
            pl.BlockSpec((tm, d), lambda i: (jnp.minimum(i, npt - 1), 0)),
            pl.BlockSpec((tm, d), lambda i: (jnp.maximum(i - npt, 0), 0)),
            pl.BlockSpec((1, d), lambda i: (0, 0)),
            pl.BlockSpec((1, d), lambda i: (0, 0)),
        ],
        out_specs=pl.BlockSpec((tm, d), lambda i: (i, 0)),
        out_shape=jax.ShapeDtypeStruct((tok.n, d), F32),
        compiler_params=_params("arbitrary"),
        name="ln_in",
    )(xp, xs, g.reshape(1, d), b.reshape(1, d))


def _inproj_kernel(x_ref, sh_ref, sc_ref, w_ref, o_ref, h_scr):
    @pl.when(pl.program_id(1) == 0)
    def _():
        h_scr[...] = (x_ref[...] * (1.0 + sc_ref[...]) + sh_ref[...]).astype(BF16)

    o_ref[...] = jnp.dot(h_scr[...], w_ref[...], preferred_element_type=F32).astype(o_ref.dtype)


def _inproj_call(tok, layer, x, mod5, w_in_bf):
    n, d = x.shape
    tm, tn = 1024, 1536
    return pl.pallas_call(
        _inproj_kernel,
        grid=(n // tm, D_IN // tn),
        in_specs=[
            pl.BlockSpec((tm, d), lambda i, j: (i, 0)),
            _mod_spec(tok, tm, layer, 0, d, 2),
            _mod_spec(tok, tm, layer, 1, d, 2),
            pl.BlockSpec((None, d, tn), lambda i, j: (layer, 0, j)),
        ],
        out_specs=pl.BlockSpec((tm, tn), lambda i, j: (i, j)),
        out_shape=jax.ShapeDtypeStruct((n, D_IN), BF16),
        scratch_shapes=[pltpu.VMEM((tm, d), BF16)],
        compiler_params=_params("arbitrary", "arbitrary"),
        name=f"inproj_l{layer}",
    )(x, mod5, mod5, w_in_bf)


def _combine_inproj_kernel(pos_ref, posn_ref, x1_ref, g2_ref, lng_ref, lnb_ref, sh_ref, sc_ref, w_ref, ys_ref,
                           x_ref, o_ref, h_scr, buf_even, buf_odd, sems, *, n_tiles, n_col):
    i = pl.program_id(0)
    j = pl.program_id(1)
    tm = x1_ref.shape[0]

    def wait(buf, s):
        pltpu.make_async_copy(ys_ref.at[pl.ds(0, tm), :], buf, sems.at[s]).wait()

    def project():
        o_ref[...] = jnp.dot(h_scr[...], w_ref[...], preferred_element_type=F32).astype(o_ref.dtype)

    @pl.when((i == 0) & (j == 0))
    def _():
        def first(t, carry):
            pltpu.make_async_copy(ys_ref.at[pl.ds(pos_ref[0, t], 1), :], buf_even.at[pl.ds(t, 1), :], sems.at[0]).start()
            return carry

        lax.fori_loop(0, tm, first, 0, unroll=DMA_ISSUE_UNROLL)

    def tile(buf_cur, s_cur, buf_nxt, s_nxt):
        @pl.when(j == 0)
        def _():
            wait(buf_cur, s_cur)
            for t in range(tm):
                pltpu.make_async_copy(ys_ref.at[pl.ds(posn_ref[0, t], 1), :], buf_nxt.at[pl.ds(t, 1), :],
                                      sems.at[s_nxt]).start()
            y = DEEPNORM_ALPHA * x1_ref[...] + (1.0 + g2_ref[...]) * buf_cur[...]
            x = _layer_norm(y, lng_ref[...], lnb_ref[...])
            x_ref[...] = x
            h_scr[...] = (x * (1.0 + sc_ref[...]) + sh_ref[...]).astype(BF16)
            project()

        @pl.when(j > 0)
        def _():
            project()

    @pl.when(i % 2 == 0)
    def _():
        tile(buf_even, 0, buf_odd, 1)

    @pl.when(i % 2 == 1)
    def _():
        tile(buf_odd, 1, buf_even, 0)

    @pl.when((i == n_tiles - 1) & (j == n_col - 1))
    def _():
        if (n_tiles - 1) % 2 == 0:
            wait(buf_odd, 1)
        else:
            wait(buf_even, 0)


def _combine_inproj_call(tok, layer, x1, ys, pos, mod5, ln_g, ln_b, w_in_bf):
    n, d = x1.shape
    tm, tn = 512, 1536
    nt = n // tm
    n_col = D_IN // tn
    prev = layer - 1
    pos3 = pos.reshape(nt, 1, tm)
    const = lambda shape: pl.BlockSpec(shape, lambda i, j: (prev,) + (0,) * (len(shape) - 1))
    return pl.pallas_call(
        functools.partial(_combine_inproj_kernel, n_tiles=nt, n_col=n_col),
        grid=(nt, n_col),
        in_specs=[
            pl.BlockSpec((None, 1, tm), lambda i, j: (i, 0, 0), memory_space=pltpu.SMEM),
            pl.BlockSpec((None, 1, tm), lambda i, j: (jnp.minimum(i + 1, nt - 1), 0, 0), memory_space=pltpu.SMEM),
            pl.BlockSpec((tm, d), lambda i, j: (i, 0)),
            _mod_spec(tok, tm, prev, 5, d, 2),
            const((None, 1, d)),
            const((None, 1, d)),
            _mod_spec(tok, tm, layer, 0, d, 2),
            _mod_spec(tok, tm, layer, 1, d, 2),
            pl.BlockSpec((None, d, tn), lambda i, j: (layer, 0, j)),
            pl.BlockSpec(memory_space=pl.ANY),
        ],
        out_specs=[
            pl.BlockSpec((tm, d), lambda i, j: (i, 0)),
            pl.BlockSpec((tm, tn), lambda i, j: (i, j)),
        ],
        out_shape=[
            jax.ShapeDtypeStruct((n, d), F32),
            jax.ShapeDtypeStruct((n, D_IN), BF16),
        ],
        scratch_shapes=[pltpu.VMEM((tm, d), BF16), pltpu.VMEM((tm, d), F32), pltpu.VMEM((tm, d), F32),
                        pltpu.SemaphoreType.DMA((2,))],
        compiler_params=_params("arbitrary", "arbitrary"),
        name=f"combine_inproj_l{layer}",
    )(pos3, pos3, x1, mod5, ln_g, ln_b, mod5, mod5, w_in_bf, ys)


def _band_offsets(radius, kw):
    return (0, -radius, ATTN_TQ - kw)


def _fill_bias(bias_ref, base, slope, dil, radius, kw):
    row = lax.broadcasted_iota(jnp.int32, (ATTN_TQ, kw), 0)
    col = lax.broadcasted_iota(jnp.int32, (ATTN_TQ, kw), 1)
    for v, off in enumerate(_band_offsets(radius, kw)):
        a = jnp.abs(col - row + off)
        bias = -slope * (a * dil).astype(F32)
        bias_ref[base + v] = jnp.where(a <= radius, bias, NEG_INF)


def _band_stages(kw, heads, q_of, k_ref, v_ref, bias_ref, sink_of, s_scr, p_scr, st_scr, finalize):
    tq = ATTN_TQ

    def qk(slot, blocks):
        for j, (_, variant, q0, k0) in enumerate(blocks):
            kwin = k_ref[pl.ds(k0, kw), :]
            for h in range(heads):
                s = lax.dot_general(q_of(h, pl.ds(q0, tq)), kwin, (((1,), (1,)), ((), ())),
                                    preferred_element_type=F32)
                s_scr[slot, j * heads + h, :, :kw] = s * ATTN_SCALE + bias_ref[3 * h + variant]

    def sm(slot, blocks):
        for e in range(len(blocks) * heads):
            s = s_scr[slot, e, :, :kw]
            m = jnp.max(s, axis=-1, keepdims=True)
            sink = None if sink_of is None else sink_of(e % heads)
            if sink is not None:
                m = jnp.maximum(m, sink)
            p = jnp.exp(s - m)
            l = jnp.sum(p, axis=-1, keepdims=True)
            if sink is not None:
                l = l + jnp.exp(sink - m)
            p_scr[slot, e, :, :kw] = p.astype(BF16)
            st_scr[slot, e, 0] = jnp.broadcast_to(m, (tq, HEAD_DIM))
            st_scr[slot, e, 1] = jnp.broadcast_to(l, (tq, HEAD_DIM))

    def pv(slot, blocks):
        for j, (i, _, _, k0) in enumerate(blocks):
            vwin = v_ref[pl.ds(k0, kw), :]
            for h in range(heads):
                e = j * heads + h
                o = jnp.dot(p_scr[slot, e, :, :kw], vwin, preferred_element_type=F32)
                finalize(i, h, o, st_scr[slot, e, 0], st_scr[slot, e, 1])

    return qk, sm, pv


def _band_attention_static(segments):
    tq = ATTN_TQ
    sched = []
    for n_blocks, grp, nblk, kw, radius, stages, before_first_pv in segments:
        assert n_blocks % grp == 0
        offs = _band_offsets(radius, kw)
        for it in range(n_blocks // grp):
            blocks = []
            for i in range(it * grp, (it + 1) * grp):
                variant = 0 if i % nblk == 0 else (2 if i % nblk == nblk - 1 else 1)
                blocks.append((i, variant, i * tq, i * tq + offs[variant]))
            sched.append((stages, blocks, before_first_pv if it == 0 else None))

    def run_pv(t):
        (_, _, pv), blocks, before = sched[t]
        if before is not None:
            before()
        pv(t % 2, blocks)

    sched[0][0][0](0, sched[0][1])
    for t, ((_, sm, _), blocks, _) in enumerate(sched):
        if t >= 1:
            run_pv(t - 1)
        sm(t % 2, blocks)
        if t + 1 < len(sched):
            sched[t + 1][0][0]((t + 1) % 2, sched[t + 1][1])
    run_pv(len(sched) - 1)


def _pipeline_scratch(entries, kw):
    return [pltpu.VMEM((2, entries, ATTN_TQ, kw), F32), pltpu.VMEM((2, entries, ATTN_TQ, kw), BF16),
            pltpu.VMEM((2, entries, 2, ATTN_TQ, HEAD_DIM), F32)]


def _attn_a_kernel(slopes_ref, buf_ref, q_ref, k_ref, v_ref, o_ref,
                   q4, k4, v4, q16, k16, v16, tmp_a, tmp_b, acc, m_s, l_s, b1, b2, b3,
                   s_scr, p_scr, st_scr, *, seq):
    del buf_ref
    tq = ATTN_TQ
    grp = ATTN_A_GROUP
    bias_refs = (b1, b2, b3)
    geom = []
    for window, dil in DILATED_PATTERNS:
        radius = window // (2 * dil)
        length = seq // dil
        geom.append((dil, radius, length, min(tq + 2 * radius, length), length // tq))
    assert [g[0] for g in geom] == [1, 4, 16]
    len4, len16 = seq // 4, seq // 16

    @pl.when(pl.program_id(1) == 0)
    def _():
        slope = slopes_ref[pl.program_id(0)]
        for (dil, radius, length, kw, nblk), bref in zip(geom, bias_refs):
            _fill_bias(bref, 0, slope, dil, radius, kw)

    for src, dst4, dst16 in ((q_ref, q4, q16), (k_ref, k4, k16), (v_ref, v4, v16)):
        tmp_a[...] = src[...].astype(F32)
        for r in range(4):
            cls = tmp_a[pl.ds(r, len4, stride=4), :]
            tmp_b[r * len4:(r + 1) * len4, :] = cls
            dst4[r * len4:(r + 1) * len4, :] = cls.astype(BF16)
        for r in range(16):
            dst16[r * len16:(r + 1) * len16, :] = tmp_b[pl.ds((r % 4) * len4 + r // 4, len16, stride=4), :].astype(BF16)

    def merge(rows, o, m, l):
        m_old = m_s[rows, :]
        m_new = jnp.maximum(m_old, m)
        a_old = jnp.exp(m_old - m_new)
        a_new = jnp.exp(m - m_new)
        acc[rows, :] = a_old * acc[rows, :] + a_new * o
        l_s[rows, :] = a_old * l_s[rows, :] + a_new * l
        m_s[rows, :] = m_new

    def fin0(i, h, o, m, l):
        rows = pl.ds(i * tq, tq)
        o_ref[rows, :] = o
        tmp_a[rows, :] = m
        tmp_b[rows, :] = l

    def to_class_major():
        for r in range(4):
            acc[r * len4:(r + 1) * len4, :] = o_ref[pl.ds(r, len4, stride=4), :]
            m_s[r * len4:(r + 1) * len4, :] = tmp_a[pl.ds(r, len4, stride=4), :]
            l_s[r * len4:(r + 1) * len4, :] = tmp_b[pl.ds(r, len4, stride=4), :]

    def fin4(i, h, o, m, l):
        merge(pl.ds(i * tq, tq), o, m, l)

    nblk16 = geom[2][4]

    def fin16(i, h, o, m, l):
        r16 = i // nblk16
        l0 = (i % nblk16) * tq
        merge(pl.ds((r16 % 4) * len4 + 4 * l0 + r16 // 4, tq, stride=4), o, m, l)

    sources = ((q_ref, k_ref, v_ref), (q4, k4, v4), (q16, k16, v16))
    finals = (fin0, fin4, fin16)
    before = (None, to_class_major, None)
    segments = []
    for (dil, radius, length, kw, nblk), (qs, ks, vs), bref, fin, pre in zip(geom, sources, bias_refs, finals, before):
        stages = _band_stages(kw, 1, lambda h, rows, qs=qs: qs[rows, :], ks, vs, bref, None,
                              s_scr, p_scr, st_scr, fin)
        segments.append((seq // tq, grp, nblk, kw, radius, stages, pre))
    _band_attention_static(segments)

    for r in range(4):
        o_ref[pl.ds(r, len4, stride=4), :] = acc[r * len4:(r + 1) * len4, :] / l_s[r * len4:(r + 1) * len4, :]


def _attn_a_call(buf, proj, slopes_a, seq, n_seq, row_block0):
    n = proj.shape[0]
    assert seq % (16 * ATTN_TQ) == 0
    bias_shapes = []
    kw_max = 0
    for window, dil in DILATED_PATTERNS:
        kw = min(ATTN_TQ + window // dil, seq // dil)
        kw_max = max(kw_max, kw)
        bias_shapes.append(pltpu.VMEM((3, ATTN_TQ, kw), F32))

    def spec(col0):
        return pl.BlockSpec((seq, HEAD_DIM), lambda h, b: (row_block0 + b, col0 + h))

    if buf is None:
        buf = jnp.zeros((8, 128), F32)
        aliases = {}
    else:
        aliases = {1: 0}
    rows_bf = pltpu.VMEM((seq, HEAD_DIM), BF16)
    rows_f32 = pltpu.VMEM((seq, HEAD_DIM), F32)
    return pl.pallas_call(
        functools.partial(_attn_a_kernel, seq=seq),
        grid=(N_HEADS_A, n_seq),
        in_specs=[
            pl.BlockSpec(memory_space=pltpu.SMEM),
            pl.BlockSpec(memory_space=pl.ANY),
            spec(0), spec(N_HEADS_A), spec(2 * N_HEADS_A),
        ],
        out_specs=pl.BlockSpec((seq, HEAD_DIM), lambda h, b: (row_block0 + b, h)),
        out_shape=jax.ShapeDtypeStruct((n, D_A + D_B), F32),
        input_output_aliases=aliases,
        scratch_shapes=[rows_bf] * 6 + [rows_f32] * 5 + bias_shapes + _pipeline_scratch(ATTN_A_GROUP, kw_max),
        compiler_params=_params("arbitrary", "arbitrary"),
        name=f"attn_a_s{seq}",
    )(slopes_a, buf, proj, proj, proj)


def _attn_b_kernel(slopes_ref, sink_ref, buf_ref, q_ref, k_ref, v_ref, o_ref, bias, s_scr, p_scr, st_scr,
                   *, seq, layer):
    del buf_ref
    tq = ATTN_TQ
    kw = tq + 2 * SWA_RADIUS
    nblk = seq // tq
    g = pl.program_id(0)

    @pl.when(pl.program_id(1) == 0)
    def _():
        for hh in range(REP_B):
            _fill_bias(bias, 3 * hh, slopes_ref[g * REP_B + hh], 1, SWA_RADIUS, kw)

    def fin(i, h, o, m, l):
        o_ref[pl.ds(i * tq, tq), h * HEAD_DIM:(h + 1) * HEAD_DIM] = o / l

    stages = _band_stages(kw, REP_B, lambda h, rows: q_ref[rows, h * HEAD_DIM:(h + 1) * HEAD_DIM], k_ref, v_ref,
                          bias, lambda h: sink_ref[layer, g * REP_B + h], s_scr, p_scr, st_scr, fin)
    _band_attention_static([(nblk, ATTN_B_GROUP, nblk, kw, SWA_RADIUS, stages, None)])


def _attn_b_call(buf, proj, slopes_b, sink_b, layer, seq, n_seq, row_block0):
    n = proj.shape[0]
    kw = ATTN_TQ + 2 * SWA_RADIUS
    assert seq % ATTN_TQ == 0 and seq >= kw
    wq = REP_B * HEAD_DIM
    q_col0 = 3 * D_A // wq
    k_col0 = (3 * D_A + D_B) // HEAD_DIM
    v_col0 = (3 * D_A + D_B + D_KV_B) // HEAD_DIM
    return pl.pallas_call(
        functools.partial(_attn_b_kernel, seq=seq, layer=layer),
        grid=(N_KV_B, n_seq),
        in_specs=[
            pl.BlockSpec(memory_space=pltpu.SMEM),
            pl.BlockSpec(memory_space=pltpu.SMEM),
            pl.BlockSpec(memory_space=pl.ANY),
            pl.BlockSpec((seq, wq), lambda g, b: (row_block0 + b, q_col0 + g)),
            pl.BlockSpec((seq, HEAD_DIM), lambda g, b: (row_block0 + b, k_col0 + g)),
            pl.BlockSpec((seq, HEAD_DIM), lambda g, b: (row_block0 + b, v_col0 + g)),
        ],
        out_specs=pl.BlockSpec((seq, wq), lambda g, b: (row_block0 + b, D_A // wq + g)),
        out_shape=jax.ShapeDtypeStruct((n, D_A + D_B), F32),
        input_output_aliases={2: 0},
        scratch_shapes=[pltpu.VMEM((3 * REP_B, ATTN_TQ, kw), F32)] + _pipeline_scratch(ATTN_B_GROUP * REP_B, kw),
        compiler_params=_params("arbitrary", "arbitrary"),
        name=f"attn_b_s{seq}_l{layer}",
    )(slopes_b, sink_b, buf, proj, proj, proj)


def _route(logits):
    lane = lax.broadcasted_iota(jnp.int32, logits.shape, 1).astype(F32)
    far = float(ROUTER_LANES)
    gl = jnp.where(lane < N_GROUPS, logits, NEG_INF)
    gmax = jnp.max(gl, axis=-1, keepdims=True)
    gsum = jnp.sum(jnp.exp(gl - gmax), axis=-1, keepdims=True)
    g_gate = 1.0 / gsum
    gidx = jnp.min(jnp.where(gl == gmax, lane, far), axis=-1, keepdims=True)
    base = N_GROUPS + N_EXPERTS_PER_GROUP * gidx
    el = jnp.where((lane >= base) & (lane < base + N_EXPERTS_PER_GROUP), logits, NEG_INF)
    m1 = jnp.max(el, axis=-1, keepdims=True)
    i1 = jnp.min(jnp.where(el == m1, lane, far), axis=-1, keepdims=True)
    el2 = jnp.where(lane == i1, NEG_INF, el)
    m2 = jnp.max(el2, axis=-1, keepdims=True)
    i2 = jnp.min(jnp.where(el2 == m2, lane, far), axis=-1, keepdims=True)
    e2 = jnp.exp(m2 - m1)
    w1 = 1.0 / (1.0 + e2)
    w2 = e2 / (1.0 + e2)
    first = lane - ROUTE_W0 + base
    weights = jnp.where(first == i1, g_gate * w1, jnp.where(first == i2, g_gate * w2, 0.0))
    return jnp.where(lane == 0.0, gidx, weights)


def _outproj_kernel(a_ref, x_ref, wo_ref, gn_ref, g1_ref, sc2_ref, sh2_ref, lng_ref, lnb_ref,
                    wr_ref, br_ref, x1_ref, rt_ref, n_even, n_odd, mix_even, mix_odd):
    s = pl.program_id(0)

    @pl.when(s == 0)
    def _():
        for ref in (n_even, n_odd, mix_even, mix_odd):
            ref[...] = jnp.zeros_like(ref)

    def body(n_cur, n_prev, mix_prev, mix_prev2):
        y = DEEPNORM_ALPHA * x_ref[...] + (1.0 + g1_ref[...]) * mix_prev2[...]
        x1 = _layer_norm(y, lng_ref[...], lnb_ref[...])
        x1_ref[...] = x1
        h2 = (x1 * (1.0 + sc2_ref[...]) + sh2_ref[...]).astype(BF16)
        rt_ref[...] = _route(jnp.dot(h2, wr_ref[...], preferred_element_type=F32) + br_ref[...])

        mix = jnp.dot(n_prev[:, :D_A], wo_ref[:D_A, :], preferred_element_type=F32)
        mix_prev[...] = mix + jnp.dot(n_prev[:, D_A:], wo_ref[D_A:, :], preferred_element_type=F32)

        n_cur[:, :D_A] = _rms_norm(a_ref[:, :D_A], gn_ref[:, :D_A]).astype(BF16)
        n_cur[:, D_A:] = _rms_norm(a_ref[:, D_A:], gn_ref[:, D_A:]).astype(BF16)

    @pl.when(s % 2 == 0)
    def _():
        body(n_even, n_odd, mix_odd, mix_even)

    @pl.when(s % 2 == 1)
    def _():
        body(n_odd, n_even, mix_even, mix_odd)


def _outproj_call(tok, layer, attn, x, mod5, w_out_bf, gn, ln_g, ln_b, w_router_bf, b_router):
    n, d = x.shape
    tm = 512
    nt = n // tm
    late = lambda i: jnp.clip(i - 2, 0, nt - 1)
    const = lambda shape: pl.BlockSpec(shape, lambda i: (layer,) + (0,) * (len(shape) - 1),
                                       pipeline_mode=pl.Buffered(1))
    mod = lambda k: pl.BlockSpec((None, None, None, 1, d),
                                 lambda i: (layer, tok.batch_of_tile(late(i), tm), k, 0, 0))
    return pl.pallas_call(
        _outproj_kernel,
        grid=(nt + 2,),
        in_specs=[
            pl.BlockSpec((tm, D_A + D_B), lambda i: (jnp.minimum(i, nt - 1), 0)),
            pl.BlockSpec((tm, d), lambda i: (late(i), 0)),
            const((None, D_A + D_B, d)),
            const((None, 1, D_A + D_B)),
            mod(2), mod(4), mod(3),
            const((None, 1, d)),
            const((None, 1, d)),
            const((None, d, ROUTER_LANES)),
            const((None, 1, ROUTER_LANES)),
        ],
        out_specs=[
            pl.BlockSpec((tm, d), lambda i: (late(i), 0)),
            pl.BlockSpec((tm, ROUTER_LANES), lambda i: (late(i), 0)),
        ],
        out_shape=[
            jax.ShapeDtypeStruct((n, d), F32),
            jax.ShapeDtypeStruct((n, ROUTER_LANES), F32),
        ],
        scratch_shapes=[pltpu.VMEM((tm, D_A + D_B), BF16), pltpu.VMEM((tm, D_A + D_B), BF16),
                        pltpu.VMEM((tm, d), F32), pltpu.VMEM((tm, d), F32)],
        compiler_params=_params("arbitrary"),
        name=f"outproj_l{layer}",
    )(attn, x, w_out_bf, gn, mod5, mod5, mod5, ln_g, ln_b, w_router_bf, b_router)


def _moe_plan(route, n):
    tm = MOE_TM
    n_tiles = n // tm + N_GROUPS
    groups = jnp.arange(N_GROUPS, dtype=jnp.int32)
    gi = route[:, 0].astype(jnp.int32)
    onehot = (gi[:, None] == groups[None, :]).astype(jnp.int32)
    csum = jnp.cumsum(onehot, axis=0)
    counts = csum[-1]
    padded = (counts + tm - 1) // tm * tm
    ends = jnp.cumsum(padded)
    base = ends - padded
    rank = jnp.sum(csum * onehot, axis=1) - 1
    pos = jnp.sum(base[None, :] * onehot, axis=1) + rank
    k = jnp.arange(N_GROUPS * tm, dtype=jnp.int32)
    kg, kr = k // tm, k % tm
    pad_slot = base[kg] + counts[kg] + kr
    pad_pos = jnp.where(counts[kg] + kr < padded[kg], pad_slot, n_tiles * tm + k)
    tile_start = jnp.arange(n_tiles, dtype=jnp.int32) * tm
    tile_group = jnp.minimum(jnp.sum((tile_start[:, None] >= ends[None, :]).astype(jnp.int32), axis=1), N_GROUPS - 1)
    n_used = (ends[-1] // tm).reshape(1)
    return pos, pad_pos, tile_group, n_used, n_tiles


def _pack_bf16_pairs(hi, lo):
    hi_bits = lax.bitcast_convert_type(hi.astype(BF16).astype(F32), jnp.uint32)
    lo_bits = lax.bitcast_convert_type(lo.astype(BF16).astype(F32), jnp.uint32)
    return hi_bits | (lo_bits >> 16)


def _unpack_bf16_pairs(words):
    hi = lax.bitcast_convert_type(words & jnp.uint32(0xFFFF0000), F32).astype(BF16)
    lo = lax.bitcast_convert_type(words << 16, F32).astype(BF16)
    return hi, lo


def _dispatch_kernel(pos_ref, pad_ref, x_ref, rt_ref, sc2_ref, sh2_ref, hs_ref, buf_a, buf_b, zero, sems, zsem):
    i = pl.program_id(0)
    n = pl.num_programs(0)
    tm, d = x_ref.shape
    half = tm // 2

    def wait_all():
        for s, buf in enumerate((buf_a, buf_b)):
            pltpu.make_async_copy(buf, hs_ref.at[pl.ds(0, half), :], sems.at[s]).wait()

    for s, buf in enumerate((buf_a, buf_b)):
        rows = slice(s * half, (s + 1) * half)

        @pl.when(i >= 1)
        def _():
            pltpu.make_async_copy(buf, hs_ref.at[pl.ds(0, half), :], sems.at[s]).wait()

        h = x_ref[rows, :] * (1.0 + sc2_ref[...]) + sh2_ref[...]
        buf[:, :d // 2] = _pack_bf16_pairs(h[:, :d // 2], h[:, d // 2:])
        buf[:, d // 2:] = lax.bitcast_convert_type(rt_ref[rows, :], jnp.uint32)
        for t in range(half):
            pltpu.make_async_copy(buf.at[pl.ds(t, 1), :], hs_ref.at[pl.ds(pos_ref[0, s * half + t], 1), :],
                                  sems.at[s]).start()

    @pl.when(i == n - 1)
    def _():
        n_pad = pad_ref.shape[0]
        zero[...] = jnp.zeros_like(zero)

        def issue_pad(t, carry):
            pltpu.make_async_copy(zero.at[pl.ds(0, 1), :], hs_ref.at[pl.ds(pad_ref[t], 1), :], zsem.at[0]).start()
            return carry

        lax.fori_loop(0, n_pad, issue_pad, 0)

        def wait_pad(t, carry):
            pltpu.make_async_copy(zero.at[pl.ds(0, 1), :], hs_ref.at[pl.ds(0, 1), :], zsem.at[0]).wait()
            return carry

        lax.fori_loop(0, n_pad, wait_pad, 0)
        wait_all()


def _dispatch_call(tok, layer, x1, route, mod5, pos, pad_pos, n_slots):
    n, d = x1.shape
    tm = 512
    w = d // 2 + ROUTER_LANES
    u32 = jnp.uint32
    return pl.pallas_call(
        _dispatch_kernel,
        grid=(n // tm,),
        in_specs=[
            pl.BlockSpec((None, 1, tm), lambda i: (i, 0, 0), memory_space=pltpu.SMEM),
            pl.BlockSpec(memory_space=pltpu.SMEM),
            pl.BlockSpec((tm, d), lambda i: (i, 0)),
            pl.BlockSpec((tm, ROUTER_LANES), lambda i: (i, 0)),
            _mod_spec(tok, tm, layer, 4, d, 1),
            _mod_spec(tok, tm, layer, 3, d, 1),
        ],
        out_specs=pl.BlockSpec(memory_space=pl.ANY),
        out_shape=jax.ShapeDtypeStruct((n_slots, w), u32),
        scratch_shapes=[pltpu.VMEM((tm // 2, w), u32), pltpu.VMEM((tm // 2, w), u32), pltpu.VMEM((8, w), u32),
                        pltpu.SemaphoreType.DMA((2,)), pltpu.SemaphoreType.DMA((1,))],
        compiler_params=_params("arbitrary"),
        name=f"moe_dispatch_l{layer}",
    )(pos.reshape(n // tm, 1, tm), pad_pos, x1, route, mod5, mod5)


def _experts_kernel(tg_ref, nu_ref, hs_ref, wg_ref, wu_ref, wd_ref, ys_ref):
    del tg_ref
    d = ys_ref.shape[-1]
    f = wd_ref.shape[0] // N_EXPERTS_PER_GROUP

    @pl.when(pl.program_id(0) < nu_ref[0])
    def _():
        h = jnp.concatenate(_unpack_bf16_pairs(hs_ref[:, :d // 2]), axis=-1)
        route = lax.bitcast_convert_type(hs_ref[:, d // 2:], F32)
        a = jnp.dot(h, wg_ref[...], preferred_element_type=F32)
        u = jnp.dot(h, wu_ref[...], preferred_element_type=F32)
        hid = (a * jax.nn.sigmoid(a)) * u
        parts = [hid[:, e * f:(e + 1) * f] * route[:, ROUTE_W0 + e:ROUTE_W0 + e + 1]
                 for e in range(N_EXPERTS_PER_GROUP)]
        ys_ref[...] = jnp.dot(jnp.concatenate(parts, axis=-1).astype(BF16), wd_ref[...],
                              preferred_element_type=F32)


def _experts_call(layer, hs, tile_group, n_used, n_tiles, wg4, wu4, wd4):
    tm = MOE_TM
    w = hs.shape[-1]
    d = wd4.shape[-1]
    ef = wg4.shape[-1]
    row = lambda i, tg, nu: (jnp.minimum(i, nu[0] - 1), 0)
    wspec = lambda shape: pl.BlockSpec(shape, lambda i, tg, nu: (layer, tg[jnp.minimum(i, nu[0] - 1)], 0, 0))
    return pl.pallas_call(
        _experts_kernel,
        grid_spec=pltpu.PrefetchScalarGridSpec(
            num_scalar_prefetch=2,
            grid=(n_tiles,),
            in_specs=[
                pl.BlockSpec((tm, w), row),
                wspec((None, None, d, ef)),
                wspec((None, None, d, ef)),
                wspec((None, None, ef, d)),
            ],
            out_specs=pl.BlockSpec((tm, d), row),
        ),
        out_shape=jax.ShapeDtypeStruct((n_tiles * tm, d), F32),
        compiler_params=_params("arbitrary"),
        name=f"moe_experts_l{layer}",
    )(tile_group, n_used, hs, wg4, wu4, wd4)


def _combine_kernel(pos_ref, posn_ref, x_ref, g2_ref, lng_ref, lnb_ref, ys_ref, op_ref, os_ref, buf, sems,
                    *, n_prompt_tiles):
    i = pl.program_id(0)
    n = pl.num_programs(0)
    tm = x_ref.shape[0]
    slot = i % 2

    def gather(p_ref, s):
        def issue(t, carry):
            pltpu.make_async_copy(ys_ref.at[pl.ds(p_ref[0, t], 1), :], buf.at[s, pl.ds(t, 1), :], sems.at[s]).start()
            return carry

        lax.fori_loop(0, tm, issue, 0, unroll=DMA_ISSUE_UNROLL)

    @pl.when(i == 0)
    def _():
        gather(pos_ref, slot)

    @pl.when(i + 1 < n)
    def _():
        gather(posn_ref, 1 - slot)

    pltpu.make_async_copy(ys_ref.at[pl.ds(0, tm), :], buf.at[slot], sems.at[slot]).wait()

    def finish():
        y = DEEPNORM_ALPHA * x_ref[...] + (1.0 + g2_ref[...]) * buf[slot]
        return _layer_norm(y, lng_ref[...], lnb_ref[...])

    @pl.when(i < n_prompt_tiles)
    def _():
        op_ref[...] = finish()

    @pl.when(i >= n_prompt_tiles)
    def _():
        os_ref[...] = finish()


def _combine_call(tok, layer, x1, ys, pos, mod5, ln_g, ln_b):
    n, d = x1.shape
    tm = 512
    nt = n // tm
    npt = tok.n_prompt // tm
    const = lambda shape: pl.BlockSpec(shape, lambda i: (layer,) + (0,) * (len(shape) - 1))
    out_specs = [
        pl.BlockSpec((tm, d), lambda i: (jnp.minimum(i, npt - 1), 0)),
        pl.BlockSpec((tm, d), lambda i: (jnp.maximum(i - npt, 0), 0)),
    ]
    out_shape = [jax.ShapeDtypeStruct((tok.n_prompt, d), F32), jax.ShapeDtypeStruct((tok.n_sample, d), F32)]
    pos3 = pos.reshape(nt, 1, tm)
    return pl.pallas_call(
        functools.partial(_combine_kernel, n_prompt_tiles=npt),
        grid=(nt,),
        in_specs=[
            pl.BlockSpec((None, 1, tm), lambda i: (i, 0, 0), memory_space=pltpu.SMEM),
            pl.BlockSpec((None, 1, tm), lambda i: (jnp.minimum(i + 1, nt - 1), 0, 0), memory_space=pltpu.SMEM),
            pl.BlockSpec((tm, d), lambda i: (i, 0)),
            _mod_spec(tok, tm, layer, 5, d, 1),
            const((None, 1, d)),
            const((None, 1, d)),
            pl.BlockSpec(memory_space=pl.ANY),
        ],
        out_specs=out_specs,
        out_shape=out_shape,
        scratch_shapes=[pltpu.VMEM((2, tm, d), F32), pltpu.SemaphoreType.DMA((2,))],
        compiler_params=_params("arbitrary"),
        name=f"moe_combine_l{layer}",
    )(pos3, pos3, x1, mod5, ln_g, ln_b, ys)


def kernel(x_prompt, x_sample, c_prompt, c_sample, ln_in_g, ln_in_b, w_ada, b_ada, w_in, sink_b, gn_a, gn_b,
           w_out, ln1_g, ln1_b, w_rg, b_rg, w_re, b_re, w_gate, w_up, w_down, ln2_g, ln2_b):
    bp, sp, d = x_prompt.shape
    bs, ss, _ = x_sample.shape
    depth = w_in.shape[0]
    tok = _Tokens(bp, sp, bs, ss)
    nb = bp + bs
    nbp = -(-nb // 8) * 8
    assert tok.n_prompt % ss == 0

    slopes_a, slopes_b = (jnp.asarray(s) for s in _alibi_slopes())
    w_in_bf = w_in.astype(BF16)
    w_out_bf = w_out.astype(BF16)
    g, e, f = N_GROUPS, N_EXPERTS_PER_GROUP, w_gate.shape[-1]
    wg4 = jnp.transpose(w_gate.astype(BF16), (0, 1, 3, 2, 4)).reshape(depth, g, d, e * f)
    wu4 = jnp.transpose(w_up.astype(BF16), (0, 1, 3, 2, 4)).reshape(depth, g, d, e * f)
    wd4 = w_down.astype(BF16).reshape(depth, g, e * f, d)
    gn = jnp.concatenate([gn_a, gn_b], axis=-1).reshape(depth, 1, D_A + D_B)
    w_router_bf = jnp.concatenate(
        [w_rg, jnp.transpose(w_re, (0, 2, 1, 3)).reshape(depth, d, g * e),
         jnp.zeros((depth, d, ROUTER_LANES - g - g * e), F32)], axis=-1).astype(BF16)
    b_router = jnp.concatenate(
        [b_rg, b_re.reshape(depth, g * e), jnp.zeros((depth, ROUTER_LANES - g - g * e), F32)],
        axis=-1).reshape(depth, 1, ROUTER_LANES)
    ln1g, ln1b = ln1_g.reshape(depth, 1, d), ln1_b.reshape(depth, 1, d)
    ln2g, ln2b = ln2_g.reshape(depth, 1, d), ln2_b.reshape(depth, 1, d)

    c_pad = jnp.concatenate([c_prompt, c_sample, jnp.zeros((nbp - nb, d), F32)], axis=0)
    mod5 = _ada_call(c_pad, w_ada, b_ada).reshape(depth, nbp, 6, 1, d)

    x = _ln_in_call(tok, x_prompt.reshape(bp * sp, d), x_sample.reshape(bs * ss, d), ln_in_g, ln_in_b)

    sample_block0 = tok.n_prompt // ss
    for layer in range(depth):
        if layer == 0:
            proj = _inproj_call(tok, layer, x, mod5, w_in_bf)
        else:
            x, proj = _combine_inproj_call(tok, layer, x1, ys, pos, mod5, ln2g, ln2b, w_in_bf)
        attn = _attn_a_call(None, proj, slopes_a, sp, bp, 0)
        attn = _attn_a_call(attn, proj, slopes_a, ss, bs, sample_block0)
        attn = _attn_b_call(attn, proj, slopes_b, sink_b, layer, sp, bp, 0)
        attn = _attn_b_call(attn, proj, slopes_b, sink_b, layer, ss, bs, sample_block0)
        x1, route = _outproj_call(tok, layer, attn, x, mod5, w_out_bf, gn, ln1g, ln1b, w_router_bf, b_router)
        pos, pad_pos, tile_group, n_used, n_tiles = _moe_plan(route, tok.n)
        hs = _dispatch_call(tok, layer, x1, route, mod5, pos, pad_pos, (n_tiles + N_GROUPS) * MOE_TM)
        ys = _experts_call(layer, hs, tile_group, n_used, n_tiles, wg4, wu4, wd4)

    y_prompt, y_sample = _combine_call(tok, depth - 1, x1, ys, pos, mod5, ln2g, ln2b)
    return y_prompt.reshape(bp, sp, d), y_sample.reshape(bs, ss, d)
```
